```python
import math
import jax, jax.numpy as jnp
from jax import lax
import numpy as np

D_MODEL = 1024
BATCH = 4
SEQ = 8192
DEPTH = 2

CHUNK = 64
N_EVEN = (DEPTH + 1) // 2
N_ODD = DEPTH // 2
EPS = 1e-6
LN_EPS = 1e-5
ROPE_THETA = 10000.0

CONV_DIM = D_MODEL
CONV_WIDTH = 31
RWKV_HEAD = 64
RWKV_HEADS = D_MODEL // RWKV_HEAD
RWKV_DIM = RWKV_HEADS * RWKV_HEAD
LORA_DIM = 64
GN_EPS = 64e-5
ATT_HEADS = 16
ATT_HEAD_DIM = 128
ATT_DIM = ATT_HEADS * ATT_HEAD_DIM
IDX_HEADS = 8
IDX_HEAD_DIM = 64
TOPK_MAX = 256
Q_BLOCK = 128

EVEN_SIZES = (CONV_DIM, CONV_DIM, CONV_DIM,
              RWKV_DIM, RWKV_DIM, RWKV_DIM,
              LORA_DIM, LORA_DIM, RWKV_DIM)
EVEN_COLS = sum(EVEN_SIZES)
EVEN_OUT = CONV_DIM + RWKV_DIM
ODD_SIZES = (ATT_DIM, ATT_HEAD_DIM, ATT_HEAD_DIM,
             IDX_HEADS * IDX_HEAD_DIM, IDX_HEAD_DIM, IDX_HEADS,
             ATT_DIM)
ODD_COLS = sum(ODD_SIZES)

kernel_name = "chunk_causal_hybrid_conv_rwkv7_dsa"


def split_cols(z, sizes):
    idx = np.cumsum(sizes)[:-1].tolist()
    return jnp.split(z, idx, axis=-1)


def rms_norm(x, g):
    xf = x.astype(jnp.float32)
    y = xf * lax.rsqrt(jnp.mean(xf * xf, axis=-1, keepdims=True) + EPS)
    return (y * g.astype(jnp.float32)).astype(x.dtype)


def layer_norm(x, w, b, eps):
    xf = x.astype(jnp.float32)
    mu = jnp.mean(xf, axis=-1, keepdims=True)
    var = jnp.mean(jnp.square(xf - mu), axis=-1, keepdims=True)
    y = (xf - mu) * lax.rsqrt(var + eps)
    return (y * w.astype(jnp.float32) + b.astype(jnp.float32)).astype(x.dtype)


def rope(x, pos):
    d = x.shape[-1]
    inv = ROPE_THETA ** (-jnp.arange(0, d, 2, dtype=jnp.float32) / d)
    ang = pos.astype(jnp.float32)[..., None] * inv
    cos = jnp.cos(ang)[:, :, None, :]
    sin = jnp.sin(ang)[:, :, None, :]
    xf = x.astype(jnp.float32)
    x1, x2 = jnp.split(xf, 2, axis=-1)
    out = jnp.concatenate([x1 * cos - x2 * sin, x2 * cos + x1 * sin], axis=-1)
    return out.astype(x.dtype)


def token_shift(z):
    return jnp.pad(z[:, :-1], ((0, 0), (1, 0), (0, 0)))


def conformer_conv(val, glu, conv_w, conv_vec):
    u = val * jax.nn.sigmoid(glu)
    y = lax.conv_general_dilated(u, conv_w.astype(u.dtype), window_strides=(1,),
                                 padding=[(CONV_WIDTH - 1, 0)],
                                 dimension_numbers=('NWC', 'WIO', 'NWC'),
                                 feature_group_count=CONV_DIM)
    y = y + conv_vec[0]
    y = layer_norm(y, conv_vec[1], conv_vec[2], LN_EPS)
    return jax.nn.silu(y)


def rwkv7_step(state, inp):
    r_t, w_t, k_t, v_t, kk_t, a_t = inp
    sa = jnp.einsum('bhvk,bhk->bhv', state, -kk_t)
    state = (state * w_t[:, :, None, :]
             + sa[..., None] * (kk_t * a_t)[:, :, None, :]
             + v_t[..., None] * k_t[:, :, None, :])
    y = jnp.einsum('bhvk,bhk->bhv', state, r_t)
    return state, y


def rwkv7_mix(r, k, v, wd, ad, mu_rkv, mu_lora, vec, w_up, a_up, r_k):
    B, S, _ = r.shape
    f32 = jnp.float32
    lerp = lambda z, mu: z + (token_shift(z) - z) * mu
    r = lerp(r, mu_rkv[0]).astype(f32)
    k = lerp(k, mu_rkv[1]).astype(f32)
    v = lerp(v, mu_rkv[2]).astype(f32)
    wd = lerp(wd, mu_lora[0]).astype(f32)
    ad = lerp(ad, mu_lora[1]).astype(f32)
    vec = vec.astype(f32)
    w0, a0, k_k, k_a, lnx_w, lnx_b = vec[0], vec[1], vec[2], vec[3], vec[4], vec[5]
    w_log = -jax.nn.softplus(-(w0 + jnp.tanh(wd) @ w_up.astype(f32))) - 0.5
    decay = jnp.exp(-jnp.exp(w_log))
    a = jax.nn.sigmoid(a0 + ad @ a_up.astype(f32))
    heads = lambda t: t.reshape(B, S, RWKV_HEADS, RWKV_HEAD)
    kk = heads(k * k_k)
    kk = kk / jnp.maximum(jnp.sqrt(jnp.sum(kk * kk, axis=-1, keepdims=True)), 1e-12)
    k = k * (1.0 + (a - 1.0) * k_a)
    r_h, w_h, k_h, v_h, a_h = heads(r), heads(decay), heads(k), heads(v), heads(a)
    xs = tuple(jnp.moveaxis(t, 1, 0) for t in (r_h, w_h, k_h, v_h, kk, a_h))
    state0 = jnp.zeros((B, RWKV_HEADS, RWKV_HEAD, RWKV_HEAD), f32)
    _, y = lax.scan(rwkv7_step, state0, xs)
    y = jnp.moveaxis(y, 0, 1)
    mu = jnp.mean(y, axis=-1, keepdims=True)
    var = jnp.mean(jnp.square(y - mu), axis=-1, keepdims=True)
    y = ((y - mu) * lax.rsqrt(var + GN_EPS)).reshape(B, S, RWKV_DIM) * lnx_w + lnx_b
    bonus = jnp.sum(r_h * k_h * r_k.astype(f32), axis=-1, keepdims=True) * v_h
    return y + bonus.reshape(B, S, RWKV_DIM)


def dsa_attention(q, k, v, qi, ki, wi, positions):
    B, S = positions.shape
    k_sel = min(TOPK_MAX, S // 4)
    nb = S // Q_BLOCK
    chunk_id = positions // CHUNK
    kf = ki.astype(jnp.float32)

    def to_blocks(t):
        return jnp.moveaxis(t.reshape((B, nb, Q_BLOCK) + t.shape[2:]), 1, 0)

    def block(args):
        qb, qib, wib, cqb = args
        idx_logits = jnp.einsum('bqhd,bsd->bqhs', qib.astype(jnp.float32), kf) * (IDX_HEAD_DIM ** -0.5)
        score = jnp.einsum('bqh,bqhs->bqs', wib.astype(jnp.float32) * (IDX_HEADS ** -0.5),
                           jax.nn.relu(idx_logits))
        adm = chunk_id[:, None, :] <= cqb[:, :, None]
        score = jnp.where(adm, score, -jnp.inf)
        top_val, top_idx = lax.top_k(score, k_sel)
        valid = jnp.isfinite(top_val)
        kg = jax.vmap(lambda kb, ib: kb[ib])(k, top_idx)
        vg = jax.vmap(lambda vb, ib: vb[ib])(v, top_idx)
        logits = jnp.einsum('bqhd,bqkd->bqhk', qb, kg).astype(jnp.float32) * (ATT_HEAD_DIM ** -0.5)
        logits = jnp.where(valid[:, :, None, :], logits, -jnp.inf)
        p = jax.nn.softmax(logits, axis=-1).astype(v.dtype)
        return jnp.einsum('bqhk,bqkd->bqhd', p, vg)

    o = lax.map(block, (to_blocks(q), to_blocks(qi), to_blocks(wi), to_blocks(chunk_id)))
    return jnp.moveaxis(o, 0, 1).reshape(B, S, ATT_DIM)


def setup_inputs(seed: int = 0) -> dict:
    key = jax.random.key(seed)
    ks = jax.random.split(key, 24)
    nrm = lambda k, shape, s: jax.random.normal(k, shape, jnp.float32) * s
    x = nrm(ks[0], (BATCH, SEQ, D_MODEL), 1.0)
    c = nrm(ks[1], (BATCH, D_MODEL), 1.0)
    offsets = jax.random.randint(ks[2], (BATCH,), 0, 64, dtype=jnp.int32) * CHUNK
    positions = offsets[:, None] + jnp.arange(SEQ, dtype=jnp.int32)[None, :]
    ada_w = nrm(ks[3], (DEPTH, D_MODEL, 3 * D_MODEL), 0.5 * D_MODEL ** -0.5)
    ada_b = nrm(ks[4], (DEPTH, 3 * D_MODEL), 0.02)
    norm_g = 1.0 + nrm(ks[5], (DEPTH, D_MODEL), 0.05)
    final_g = 1.0 + nrm(ks[6], (D_MODEL,), 0.05)
    even_w_in = nrm(ks[7], (N_EVEN, D_MODEL, EVEN_COLS), D_MODEL ** -0.5)
    even_w_out = nrm(ks[8], (N_EVEN, EVEN_OUT, D_MODEL), EVEN_OUT ** -0.5)
    conv_w = nrm(ks[9], (N_EVEN, CONV_WIDTH, 1, CONV_DIM), CONV_WIDTH ** -0.5)
    kc = jax.random.split(ks[10], 3)
    conv_vec = jnp.stack([nrm(kc[0], (N_EVEN, CONV_DIM), 0.02),
                          1.0 + nrm(kc[1], (N_EVEN, CONV_DIM), 0.05),
                          nrm(kc[2], (N_EVEN, CONV_DIM), 0.02)], axis=1)
    rwkv_mu_rkv = jax.random.uniform(ks[11], (N_EVEN, 3, RWKV_DIM), jnp.float32)
    rwkv_mu_lora = jax.random.uniform(ks[12], (N_EVEN, 2, LORA_DIM), jnp.float32)
    kv = jax.random.split(ks[13], 6)
    rwkv_vec = jnp.stack([
        jax.random.uniform(kv[0], (N_EVEN, RWKV_DIM), jnp.float32, -4.0, 1.0),
        nrm(kv[1], (N_EVEN, RWKV_DIM), 0.1),
        0.85 + nrm(kv[2], (N_EVEN, RWKV_DIM), 0.05),
        1.0 + nrm(kv[3], (N_EVEN, RWKV_DIM), 0.05),
        1.0 + nrm(kv[4], (N_EVEN, RWKV_DIM), 0.05),
        nrm(kv[5], (N_EVEN, RWKV_DIM), 0.02)], axis=1)
    rwkv_w_up = nrm(ks[14], (N_EVEN, LORA_DIM, RWKV_DIM), 0.5 * LORA_DIM ** -0.5)
    rwkv_a_up = nrm(ks[15], (N_EVEN, LORA_DIM, RWKV_DIM), 0.5 * LORA_DIM ** -0.5)
    rwkv_r_k = nrm(ks[16], (N_EVEN, RWKV_HEADS, RWKV_HEAD), 0.1)
    odd_w_in = nrm(ks[17], (N_ODD, D_MODEL, ODD_COLS), D_MODEL ** -0.5)
    odd_w_out = nrm(ks[18], (N_ODD, ATT_DIM, D_MODEL), ATT_DIM ** -0.5)
    return {"x": x, "c": c, "positions": positions, "ada_w": ada_w, "ada_b": ada_b,
            "norm_g": norm_g, "final_g": final_g, "even_w_in": even_w_in,
            "even_w_out": even_w_out, "conv_w": conv_w, "conv_vec": conv_vec,
            "rwkv_mu_rkv": rwkv_mu_rkv, "rwkv_mu_lora": rwkv_mu_lora, "rwkv_vec": rwkv_vec,
            "rwkv_w_up": rwkv_w_up, "rwkv_a_up": rwkv_a_up, "rwkv_r_k": rwkv_r_k,
            "odd_w_in": odd_w_in, "odd_w_out": odd_w_out}


def reference(x, c, positions, ada_w, ada_b, norm_g, final_g, even_w_in, even_w_out,
              conv_w, conv_vec, rwkv_mu_rkv, rwkv_mu_lora, rwkv_vec, rwkv_w_up, rwkv_a_up,
              rwkv_r_k, odd_w_in, odd_w_out):
    B, S, _ = x.shape
    cond = jax.nn.silu(c)
    for l in range(DEPTH):
        mod = cond @ ada_w[l] + ada_b[l]
        shift, scale, gate = jnp.split(mod, 3, axis=-1)
        h = rms_norm(x, norm_g[l]) * (1.0 + scale[:, None, :]) + shift[:, None, :]
        j = l // 2
        if l % 2 == 0:
            z = h @ even_w_in[j]
            a_val, a_glu, a_gate, r, k, v, wd, ad, b_gate = split_cols(z, EVEN_SIZES)
            ya = conformer_conv(a_val, a_glu, conv_w[j], conv_vec[j]) * jax.nn.silu(a_gate)
            yb = rwkv7_mix(r, k, v, wd, ad, rwkv_mu_rkv[j], rwkv_mu_lora[j], rwkv_vec[j],
                           rwkv_w_up[j], rwkv_a_up[j], rwkv_r_k[j]).astype(h.dtype)
            yb = yb * jax.nn.silu(b_gate)
            out = jnp.concatenate([ya, yb], axis=-1) @ even_w_out[j]
        else:
            z = h @ odd_w_in[j]
            q, k, v, qi, ki, wi, g = split_cols(z, ODD_SIZES)
            q = rope(q.reshape(B, S, ATT_HEADS, ATT_HEAD_DIM), positions)
            k = rope(k[:, :, None, :], positions)[:, :, 0, :]
            qi = rope(qi.reshape(B, S, IDX_HEADS, IDX_HEAD_DIM), positions)
            ki = rope(ki[:, :, None, :], positions)[:, :, 0, :]
            o = dsa_attention(q, k, v, qi, ki, wi, positions)
            out = (o * jax.nn.silu(g)) @ odd_w_out[j]
        x = x + gate[:, None, :] * out
    return rms_norm(x, final_g)
```

```python
import functools
import math

import jax
import jax.numpy as jnp
import numpy as np
from jax import lax
from jax.experimental import pallas as pl
from jax.experimental.pallas import tpu as pltpu

F32 = jnp.float32
BF16 = jnp.bfloat16
I32 = jnp.int32

CHUNK = 64
EPS = 1e-6
LN_EPS = 1e-5
ROPE_THETA = 10000.0
CONV_WIDTH = 31
CONV_HALO = 32
RWKV_HEAD = 64
LORA_DIM = 64
GN_EPS = 64e-5
ATT_HEADS = 16
ATT_HEAD_DIM = 128
IDX_HEADS = 8
IDX_HEAD_DIM = 64
TOPK_MAX = 256
Q_BLOCK = 128
KEY_TILE = 256
QUAD = 256
HEADS_PER_QUAD = QUAD // RWKV_HEAD
RWKV_CHUNK = QUAD // HEADS_PER_QUAD
NEG_BIG = -1e30
INT_MIN = -(2 ** 31)
VMEM_LIMIT = 56 * 1024 * 1024
PROJ_ROWS = 1024
OUT_ROWS = 512
CONV_ROWS = 256
ROPE_ROWS = 512


def _cparams(sem):
    return pltpu.CompilerParams(dimension_semantics=sem, vmem_limit_bytes=VMEM_LIMIT)


def _split_bf16(x, n):
    if x.dtype == BF16:
        return [x]
    parts = []
    rem = x
    for i in range(n):
        p = rem.astype(BF16)
        parts.append(p)
        if i + 1 < n:
            rem = rem - p.astype(F32)
    return parts


def _mm(a, b, pa=1, pb=1, trans_b=False):
    ap = _split_bf16(a, pa)
    bp = _split_bf16(b, pb)
    dn = (((1,), (1 if trans_b else 0,)), ((), ()))
    order = max(len(ap), len(bp))
    acc = None
    for i, x in enumerate(ap):
        for j, y in enumerate(bp):
            if i + j >= order:
                continue
            t = lax.dot_general(x, y, dn, preferred_element_type=F32)
            acc = t if acc is None else acc + t
    return acc


def _sigmoid(x):
    return 1.0 / (1.0 + jnp.exp(-x))


def _silu(x):
    return x * _sigmoid(x)


def _mod_kernel(c_ref, w_ref, b_ref, o_ref):
    cond = _silu(c_ref[...])
    o_ref[0] = _mm(cond, w_ref[0], 2, 2) + b_ref[0]


def _modulation(c, ada_w, ada_b):
    depth, d, n = ada_w.shape
    b = c.shape[0]
    tn = 1024
    return pl.pallas_call(
        _mod_kernel,
        grid=(depth, n // tn),
        in_specs=[pl.BlockSpec((b, d), lambda l, j: (0, 0)),
                  pl.BlockSpec((1, d, tn), lambda l, j: (l, 0, j)),
                  pl.BlockSpec((1, 1, tn), lambda l, j: (l, 0, j))],
        out_specs=pl.BlockSpec((1, b, tn), lambda l, j: (l, 0, j)),
        out_shape=jax.ShapeDtypeStruct((depth, b, n), F32),
        compiler_params=_cparams(("arbitrary", "arbitrary")),
        name="adaln_mod",
    )(c, ada_w, ada_b.reshape(depth, 1, n))


def _norm_proj_kernel(x_ref, g_ref, sc_ref, sh_ref, wm_ref, ws_ref, zm_ref, zs_ref, h_scr):
    @pl.when(pl.program_id(2) == 0)
    def _():
        x = x_ref[0]
        ms = jnp.mean(x * x, axis=-1, keepdims=True)
        y = x * lax.rsqrt(ms + EPS) * g_ref[...]
        h = (y * (1.0 + sc_ref[0]) + sh_ref[0]).astype(BF16)
        h_scr[...] = h
        zs_ref[0] = jnp.dot(h, ws_ref[...], preferred_element_type=F32)

    zm_ref[0] = jnp.dot(h_scr[...], wm_ref[...], preferred_element_type=F32)


def _norm_proj(x, g, scale, shift, w_main, w_small, tm, tn):
    b, s, d = x.shape
    nm = w_main.shape[1]
    ns = w_small.shape[1]
    return pl.pallas_call(
        _norm_proj_kernel,
        grid=(b, s // tm, nm // tn),
        in_specs=[pl.BlockSpec((1, tm, d), lambda bi, i, j: (bi, i, 0)),
                  pl.BlockSpec((1, d), lambda bi, i, j: (0, 0)),
                  pl.BlockSpec((1, 1, d), lambda bi, i, j: (bi, 0, 0)),
                  pl.BlockSpec((1, 1, d), lambda bi, i, j: (bi, 0, 0)),
                  pl.BlockSpec((d, tn), lambda bi, i, j: (0, j)),
                  pl.BlockSpec((d, ns), lambda bi, i, j: (0, 0))],
        out_specs=[pl.BlockSpec((1, tm, tn), lambda bi, i, j: (bi, i, j)),
                   pl.BlockSpec((1, tm, ns), lambda bi, i, j: (bi, i, 0))],
        out_shape=[jax.ShapeDtypeStruct((b, s, nm), F32),
                   jax.ShapeDtypeStruct((b, s, ns), F32)],
        scratch_shapes=[pltpu.VMEM((tm, d), BF16)],
        compiler_params=_cparams(("arbitrary", "arbitrary", "arbitrary")),
        name="norm_proj",
    )(x, g.reshape(1, d), scale.reshape(b, 1, d), shift.reshape(b, 1, d), w_main, w_small)


def _conv_kernel(val_ref, glu_ref, gate_ref, w_ref, vec_ref, o_ref, u_scr):
    t = val_ref.shape[1]

    @pl.when(pl.program_id(1) == 0)
    def _():
        u_scr[0:CONV_HALO, :] = jnp.zeros((CONV_HALO, u_scr.shape[1]), F32)

    @pl.when(pl.program_id(1) > 0)
    def _():
        u_scr[0:CONV_HALO, :] = u_scr[t:t + CONV_HALO, :]

    u_scr[CONV_HALO:CONV_HALO + t, :] = val_ref[0] * _sigmoid(glu_ref[0])
    first = CONV_HALO - (CONV_WIDTH - 1)
    acc = w_ref[0:1, :] * u_scr[pl.ds(first, t), :]
    for j in range(1, CONV_WIDTH):
        acc = acc + w_ref[j:j + 1, :] * u_scr[pl.ds(first + j, t), :]
    y = acc + vec_ref[0:1, :]
    mu = jnp.mean(y, axis=-1, keepdims=True)
    yc = y - mu
    var = jnp.mean(yc * yc, axis=-1, keepdims=True)
    yn = yc * lax.rsqrt(var + LN_EPS) * vec_ref[1:2, :] + vec_ref[2:3, :]
    o_ref[0] = (_silu(yn) * _silu(gate_ref[0])).astype(o_ref.dtype)


def _conv_branch(zm, conv_w, conv_vec, cols, t):
    b, s, _ = zm.shape
    c = conv_w.shape[1]
    cv, cg, cs = cols
    return pl.pallas_call(
        _conv_kernel,
        grid=(b, s // t),
        in_specs=[pl.BlockSpec((1, t, c), lambda bi, i: (bi, i, cv)),
                  pl.BlockSpec((1, t, c), lambda bi, i: (bi, i, cg)),
                  pl.BlockSpec((1, t, c), lambda bi, i: (bi, i, cs)),
                  pl.BlockSpec((CONV_WIDTH, c), lambda bi, i: (0, 0)),
                  pl.BlockSpec((3, c), lambda bi, i: (0, 0))],
        out_specs=pl.BlockSpec((1, t, c), lambda bi, i: (bi, i, 0)),
        out_shape=jax.ShapeDtypeStruct((b, s, c), BF16),
        scratch_shapes=[pltpu.VMEM((CONV_HALO + t, c), F32)],
        compiler_params=_cparams(("arbitrary", "arbitrary")),
        name="conformer_conv",
    )(zm, zm, zm, conv_w, conv_vec)


def _token_shift(x, carry_row):
    rolled = pltpu.roll(x, 1, axis=0)
    row = lax.broadcasted_iota(I32, x.shape, 0)
    return jnp.where(row == 0, carry_row, rolled)


def _rwkv_kernel(r_ref, k_ref, v_ref, g_ref, lo_ref, mu_ref, mul_ref, vec_ref, rk_ref, wcat_ref,
                 tri_ref, ones_ref, o_ref, carry_scr, carryl_scr, h_scr, y_scr):
    c = r_ref.shape[1]
    dim = r_ref.shape[2]
    nquad = dim // QUAD

    @pl.when(pl.program_id(1) == 0)
    def _():
        carry_scr[...] = jnp.zeros(carry_scr.shape, F32)
        carryl_scr[...] = jnp.zeros(carryl_scr.shape, F32)
        h_scr[...] = jnp.zeros(h_scr.shape, F32)

    def lerp(idx, ref, mu_row):
        x = ref[0]
        prev = _token_shift(x, carry_scr[idx, 0:1, :])
        carry_scr[idx, 0:1, :] = x[c - 1:c, :]
        return x + (prev - x) * mu_row

    r = lerp(0, r_ref, mu_ref[0:1, :])
    k = lerp(1, k_ref, mu_ref[1:2, :])
    v = lerp(2, v_ref, mu_ref[2:3, :])
    lo0 = lo_ref[0]
    lo_prev = _token_shift(lo0, carryl_scr[0:1, :])
    carryl_scr[0:1, :] = lo0[c - 1:c, :]
    lo = lo0 + (lo_prev - lo0) * mul_ref[...]
    lane = lax.broadcasted_iota(I32, lo.shape, 1)
    lo_act = jnp.where(lane < LORA_DIM, jnp.tanh(lo), lo)
    pre = _mm(lo_act, wcat_ref[...], 2, 2)
    w_pre = pre[:, :dim] + vec_ref[0:1, :]
    a_pre = pre[:, dim:] + vec_ref[1:2, :]
    sp = jnp.maximum(-w_pre, 0.0) + jnp.log(1.0 + jnp.exp(-jnp.abs(w_pre)))
    logw = -jnp.exp(-sp - 0.5)
    a = _sigmoid(a_pre)

    ones_bd = ones_ref[...]

    def head_sum(x):
        return jnp.concatenate(
            [_mm(x[:, q * QUAD:(q + 1) * QUAD], ones_bd, 2, 1) for q in range(nquad)], axis=1)

    kk = k * vec_ref[2:3, :]
    kk = kk / jnp.maximum(jnp.sqrt(head_sum(kk * kk)), 1e-12)
    kmod = k * (1.0 + (a - 1.0) * vec_ref[3:4, :])
    kka = kk * a

    cum = _mm(tri_ref[...], logw, 1, 3)
    tot = cum[c - 1:c, :]
    e_neg = jnp.exp(-cum)
    e_rem = jnp.exp(tot - cum)
    at = -kk * jnp.exp(cum - logw)
    bt = kka * e_neg
    kt = kmod * e_neg
    rt = r * jnp.exp(cum)
    bh = kka * e_rem
    kh = kmod * e_rem
    gam = jnp.exp(tot)

    lane_h = lax.broadcasted_iota(I32, (c, QUAD), 1) // RWKV_HEAD
    n4 = HEADS_PER_QUAD * c
    row = lax.broadcasted_iota(I32, (n4, n4), 0)
    col = lax.broadcasted_iota(I32, (n4, n4), 1)
    strict = row > col
    incl = row >= col
    diag = row == col

    def stack(z):
        return jnp.concatenate([jnp.where(lane_h == h, z, 0.0) for h in range(HEADS_PER_QUAD)], axis=0)

    def unstack(zs):
        out = zs[0:c]
        for h in range(1, HEADS_PER_QUAD):
            out = out + zs[h * c:(h + 1) * c]
        return out

    for q in range(nquad):
        sl = slice(q * QUAD, (q + 1) * QUAD)
        sa = stack(at[:, sl])
        sv = stack(v[:, sl])
        lhs = jnp.concatenate([sa, stack(rt[:, sl])], axis=0)
        rhs = jnp.concatenate([stack(bt[:, sl]), stack(kt[:, sl])], axis=0)
        aa = _mm(lhs, rhs, 1, 1, trans_b=True)
        n_ab = jnp.where(strict, aa[0:n4, 0:n4], 0.0)
        a_ak = jnp.where(strict, aa[0:n4, n4:], 0.0)
        a_rb = jnp.where(incl, aa[n4:, 0:n4], 0.0)
        a_rk = jnp.where(incl, aa[n4:, n4:], 0.0)
        tm = jnp.where(diag, 1.0, n_ab)
        p = n_ab
        steps = int(math.log2(c)) - 1
        for _ in range(steps):
            p = _mm(p, p)
            tm = tm + _mm(tm, p)
        ta = _mm(tm, sa)
        tav = _mm(tm, _mm(a_ak, sv))
        bs_t = stack(bh[:, sl]).T
        ks_t = stack(kh[:, sl]).T
        m_low = _mm(bs_t, ta)
        g_new = _mm(bs_t, tav) + _mm(ks_t, sv)
        qmat = rt[:, sl] + unstack(_mm(a_rb, ta))
        y_in = unstack(_mm(a_rb, tav) + _mm(a_rk, sv))
        h0 = h_scr[q]
        gam_col = jnp.sum(jnp.where(diag, gam[:, sl], 0.0), axis=1, keepdims=True)
        y_scr[:, sl] = y_in + _mm(qmat, h0)
        h_scr[q] = gam_col * h0 + _mm(m_low, h0) + g_new

    y = y_scr[...]
    inv_n = 1.0 / RWKV_HEAD
    mu = head_sum(y) * inv_n
    yc = y - mu
    var = head_sum(yc * yc) * inv_n
    yn = yc * lax.rsqrt(var + GN_EPS) * vec_ref[4:5, :] + vec_ref[5:6, :]
    bonus = head_sum(r * kmod * rk_ref[...]) * v
    o_ref[0] = ((yn + bonus) * _silu(g_ref[0])).astype(o_ref.dtype)


def _rwkv_branch(zm, zs, mu_rkv, mu_lora, vec, w_up, a_up, r_k, cols):
    b, s, _ = zm.shape
    dim = mu_rkv.shape[1]
    c = RWKV_CHUNK
    cr, ck, cv, cg = cols
    zero = jnp.zeros((LORA_DIM, dim), F32)
    wcat = jnp.concatenate([jnp.concatenate([w_up, zero], axis=1),
                            jnp.concatenate([zero, a_up], axis=1)], axis=0)
    tri = jnp.asarray(np.tril(np.ones((c, c), np.float32)), BF16)
    hid = np.arange(QUAD) // RWKV_HEAD
    ones_bd = jnp.asarray((hid[:, None] == hid[None, :]).astype(np.float32), BF16)
    const = lambda shape: pl.BlockSpec(shape, lambda bi, i: (0,) * len(shape))
    return pl.pallas_call(
        _rwkv_kernel,
        grid=(b, s // c),
        in_specs=[pl.BlockSpec((1, c, dim), lambda bi, i: (bi, i, cr)),
                  pl.BlockSpec((1, c, dim), lambda bi, i: (bi, i, ck)),
                  pl.BlockSpec((1, c, dim), lambda bi, i: (bi, i, cv)),
                  pl.BlockSpec((1, c, dim), lambda bi, i: (bi, i, cg)),
                  pl.BlockSpec((1, c, 2 * LORA_DIM), lambda bi, i: (bi, i, 0)),
                  const((3, dim)), const((1, 2 * LORA_DIM)), const((6, dim)), const((1, dim)),
                  const((2 * LORA_DIM, 2 * dim)), const((c, c)), const((QUAD, QUAD))],
        out_specs=pl.BlockSpec((1, c, dim), lambda bi, i: (bi, i, 0)),
        out_shape=jax.ShapeDtypeStruct((b, s, dim), BF16),
        scratch_shapes=[pltpu.VMEM((3, 8, dim), F32),
                        pltpu.VMEM((8, 2 * LORA_DIM), F32),
                        pltpu.VMEM((dim // QUAD, QUAD, QUAD), F32),
                        pltpu.VMEM((c, dim), F32)],
        compiler_params=_cparams(("arbitrary", "arbitrary")),
        name="rwkv7_chunk",
    )(zm, zm, zm, zm, zs, mu_rkv, mu_lora.reshape(1, 2 * LORA_DIM), vec, r_k.reshape(1, dim),
      wcat, tri, ones_bd)


def _out_proj_kernel(*refs, n_in, final):
    y_refs = refs[:n_in]
    w_refs = refs[n_in:2 * n_in]
    x_ref, gate_ref = refs[2 * n_in], refs[2 * n_in + 1]
    o_ref = refs[-1]
    acc = jnp.dot(y_refs[0][0], w_refs[0][...], preferred_element_type=F32)
    for y_ref, w_ref in zip(y_refs[1:], w_refs[1:]):
        acc = acc + jnp.dot(y_ref[0], w_ref[...], preferred_element_type=F32)
    xn = x_ref[0] + gate_ref[0] * acc
    if final:
        fg_ref = refs[2 * n_in + 2]
        ms = jnp.mean(xn * xn, axis=-1, keepdims=True)
        xn = xn * lax.rsqrt(ms + EPS) * fg_ref[...]
    o_ref[0] = xn


def _out_proj(ys, ws, x, gate, final_g, tm):
    b, s, d = x.shape
    n_in = len(ys)
    final = final_g is not None
    in_specs = [pl.BlockSpec((1, tm, y.shape[2]), lambda bi, i: (bi, i, 0)) for y in ys]
    in_specs += [pl.BlockSpec(w.shape, lambda bi, i: (0, 0)) for w in ws]
    in_specs += [pl.BlockSpec((1, tm, d), lambda bi, i: (bi, i, 0)),
                 pl.BlockSpec((1, 1, d), lambda bi, i: (bi, 0, 0))]
    args = list(ys) + list(ws) + [x, gate.reshape(b, 1, d)]
    if final:
        in_specs.append(pl.BlockSpec((1, d), lambda bi, i: (0, 0)))
        args.append(final_g.reshape(1, d))
    return pl.pallas_call(
        functools.partial(_out_proj_kernel, n_in=n_in, final=final),
        grid=(b, s // tm),
        in_specs=in_specs,
        out_specs=pl.BlockSpec((1, tm, d), lambda bi, i: (bi, i, 0)),
        out_shape=jax.ShapeDtypeStruct((b, s, d), F32),
        compiler_params=_cparams(("arbitrary", "arbitrary")),
        name="out_proj_final" if final else "out_proj",
    )(*args)


def _rope128(x, cos, sin_signed):
    return x * cos + pltpu.roll(x, 64, axis=1) * sin_signed


def _rope64_pair(x, cos, sin_signed):
    lane = lax.broadcasted_iota(I32, x.shape, 1)
    first_half = (lane % IDX_HEAD_DIM) < (IDX_HEAD_DIM // 2)
    partner = jnp.where(first_half, pltpu.roll(x, 96, axis=1), pltpu.roll(x, 32, axis=1))
    return x * cos + partner * sin_signed


def _rope_kv_kernel(zs_ref, pos_ref, inv_ref, sgn_ref, k_ref, v_ref, ki_ref, tab_ref):
    posf = pos_ref[0].astype(F32)
    ang128 = posf * inv_ref[0:1, :]
    ang64 = posf * inv_ref[1:2, :]
    c128 = jnp.cos(ang128)
    s128 = jnp.sin(ang128) * sgn_ref[0:1, :]
    c64 = jnp.cos(ang64)
    s64 = jnp.sin(ang64) * sgn_ref[1:2, :]
    tab_ref[0, 0] = c128
    tab_ref[0, 1] = s128
    tab_ref[0, 2] = c64
    tab_ref[0, 3] = s64
    zs = zs_ref[0]
    k_ref[0] = _rope128(zs[:, 0:128], c128, s128).astype(BF16)
    v_ref[0] = zs[:, 128:256].astype(BF16)
    ki = _rope64_pair(zs[:, 256:384], c64, s64)
    lane = lax.broadcasted_iota(I32, ki.shape, 1)
    ki_ref[0] = jnp.where(lane < IDX_HEAD_DIM, ki, pltpu.roll(ki, 64, axis=1)).astype(BF16)


def _rope_kv(zs, positions, t):
    b, s, ns = zs.shape
    inv128 = ROPE_THETA ** (-jnp.arange(0, ATT_HEAD_DIM, 2, dtype=F32) / ATT_HEAD_DIM)
    inv64 = ROPE_THETA ** (-jnp.arange(0, IDX_HEAD_DIM, 2, dtype=F32) / IDX_HEAD_DIM)
    inv = jnp.stack([jnp.tile(inv128, 2), jnp.tile(inv64, 4)])
    sgn128 = np.where(np.arange(128) < 64, -1.0, 1.0)
    sgn64 = np.where((np.arange(128) % 64) < 32, -1.0, 1.0)
    sgn = jnp.asarray(np.stack([sgn128, sgn64]), F32)
    blk = lambda w: pl.BlockSpec((1, t, w), lambda bi, i: (bi, i, 0))
    return pl.pallas_call(
        _rope_kv_kernel,
        grid=(b, s // t),
        in_specs=[blk(ns), blk(1),
                  pl.BlockSpec((2, 128), lambda bi, i: (0, 0)),
                  pl.BlockSpec((2, 128), lambda bi, i: (0, 0))],
        out_specs=[blk(128), blk(128), blk(128),
                   pl.BlockSpec((1, 4, t, 128), lambda bi, i: (bi, 0, i, 0))],
        out_shape=[jax.ShapeDtypeStruct((b, s, 128), BF16),
                   jax.ShapeDtypeStruct((b, s, 128), BF16),
                   jax.ShapeDtypeStruct((b, s, 128), BF16),
                   jax.ShapeDtypeStruct((b, 4, s, 128), F32)],
        compiler_params=_cparams(("arbitrary", "arbitrary")),
        name="rope_kv",
    )(zs, positions.reshape(b, s, 1), inv, sgn)


def _dsa_kernel(q_ref, g_ref, qi_ref, zs_ref, tab_ref, qpos_ref, kpos_ref, k_ref, v_ref, ki_ref,
                upper_ref, o_ref, qs_scr, qis_scr, wb_scr, key_scr, acc_scr, *, k_sel):
    tq = q_ref.shape[1]
    qb = pl.program_id(1)
    nk = (qb * tq + tq + KEY_TILE - 1) // KEY_TILE

    c128, s128 = tab_ref[0, 0], tab_ref[0, 1]
    c64, s64 = tab_ref[0, 2], tab_ref[0, 3]
    q_scale = ATT_HEAD_DIM ** -0.5
    for h in range(ATT_HEADS):
        qh = _rope128(q_ref[0, :, h * 128:(h + 1) * 128], c128, s128) * q_scale
        qs_scr[h * tq:(h + 1) * tq, :] = qh.astype(BF16)
    lane = lax.broadcasted_iota(I32, (tq, 128), 1)
    zs = zs_ref[0]
    w_scale = (IDX_HEADS ** -0.5) * (IDX_HEAD_DIM ** -0.5)
    w_off = 256 + IDX_HEAD_DIM
    for p in range(IDX_HEADS // 2):
        pair = _rope64_pair(qi_ref[0, :, p * 128:(p + 1) * 128], c64, s64)
        qis_scr[(2 * p) * tq:(2 * p + 1) * tq, :] = jnp.where(lane < 64, pair, 0.0).astype(BF16)
        qis_scr[(2 * p + 1) * tq:(2 * p + 2) * tq, :] = jnp.where(lane >= 64, pair, 0.0).astype(BF16)
    for h in range(IDX_HEADS):
        wcol = zs[:, w_off + h:w_off + h + 1] * w_scale
        wb_scr[h * tq:(h + 1) * tq, :] = jnp.broadcast_to(wcol, (tq, 128))

    qchunk = qpos_ref[0] >> 6

    def score_body(j, carry):
        start = pl.multiple_of(j * KEY_TILE, KEY_TILE)
        kt = ki_ref[0, pl.ds(start, KEY_TILE), :]
        sc = _mm(qis_scr[...], kt, trans_b=True)
        sc = jnp.maximum(sc, 0.0) * jnp.concatenate([wb_scr[...]] * (KEY_TILE // 128), axis=1)
        tot = sc[0:tq]
        for h in range(1, IDX_HEADS):
            tot = tot + sc[h * tq:(h + 1) * tq]
        kchunk = kpos_ref[0, pl.ds(j, 1), :] >> 6
        tot = jnp.where(kchunk <= qchunk, tot, -jnp.inf)
        bits = pltpu.bitcast(tot, I32)
        key_scr[j] = bits ^ ((bits >> 31) & 0x7FFFFFFF)
        return carry

    lax.fori_loop(0, nk, score_body, 0)
    neg_key = INT_MIN + 0x7FFFFF

    def count(pred):
        def body(j, acc):
            return acc + jnp.where(pred(key_scr[j]), 1, 0)
        acc = lax.fori_loop(0, nk, body, jnp.zeros((tq, KEY_TILE), I32))
        return jnp.sum(acc, axis=1, keepdims=True)

    def bit_body(i, thr):
        cand = thr + jnp.left_shift(jnp.int32(1), 31 - i)
        cnt = count(lambda key: key >= cand)
        return jnp.where(cnt >= k_sel, cand, thr)

    thr = lax.fori_loop(0, 32, bit_body, jnp.full((tq, 1), INT_MIN, I32))
    need = (k_sel - count(lambda key: key > thr)).astype(F32)

    acc_scr[...] = jnp.zeros(acc_scr.shape, F32)
    upper = upper_ref[...]
    rows = ATT_HEADS * tq

    def att_body(j, carry):
        m, l, eqc = carry
        key = key_scr[j]
        eq = key == thr
        eqf = jnp.where(eq, 1.0, 0.0)
        before = _mm(eqf.astype(BF16), upper) + eqc
        take = jnp.where(key > thr, 1.0, jnp.where(before < need, eqf, 0.0))
        bias = jnp.where((take > 0.0) & (key > neg_key), 0.0, NEG_BIG)
        start = pl.multiple_of(j * KEY_TILE, KEY_TILE)
        kt = k_ref[0, pl.ds(start, KEY_TILE), :]
        s = _mm(qs_scr[...], kt, trans_b=True) + jnp.concatenate([bias] * ATT_HEADS, axis=0)
        m_new = jnp.maximum(m, jnp.max(s, axis=1, keepdims=True))
        alpha = jnp.exp(m - m_new)
        p = jnp.exp(s - m_new)
        l_new = alpha * l + jnp.sum(p, axis=1, keepdims=True)
        vt = v_ref[0, pl.ds(start, KEY_TILE), :]
        acc_scr[...] = alpha * acc_scr[...] + _mm(p.astype(BF16), vt)
        return m_new, l_new, eqc + jnp.sum(eqf, axis=1, keepdims=True)

    init = (jnp.full((rows, 1), NEG_BIG, F32), jnp.zeros((rows, 1), F32), jnp.zeros((tq, 1), F32))
    _, l_fin, _ = lax.fori_loop(0, nk, att_body, init)
    o = acc_scr[...] / l_fin
    for h in range(ATT_HEADS):
        sl = slice(h * 128, (h + 1) * 128)
        o_ref[0, :, sl] = (o[h * tq:(h + 1) * tq, :] * _silu(g_ref[0, :, sl])).astype(o_ref.dtype)


def _dsa(zm, zs, tabs, positions, k_rot, v_b, ki2):
    b, s, _ = zm.shape
    tq = Q_BLOCK
    att = ATT_HEADS * ATT_HEAD_DIM
    idx = IDX_HEADS * IDX_HEAD_DIM
    k_sel = min(TOPK_MAX, s // 4)
    nkt = s // KEY_TILE
    upper = jnp.asarray(np.triu(np.ones((KEY_TILE, KEY_TILE), np.float32), 1), BF16)
    full = lambda w: pl.BlockSpec((1, s, w), lambda bi, i: (bi, 0, 0))
    return pl.pallas_call(
        functools.partial(_dsa_kernel, k_sel=k_sel),
        grid=(b, s // tq),
        in_specs=[pl.BlockSpec((1, tq, att), lambda bi, i: (bi, i, 0)),
                  pl.BlockSpec((1, tq, att), lambda bi, i: (bi, i, 1)),
                  pl.BlockSpec((1, tq, idx), lambda bi, i: (bi, i, 2 * att // idx)),
                  pl.BlockSpec((1, tq, zs.shape[2]), lambda bi, i: (bi, i, 0)),
                  pl.BlockSpec((1, 4, tq, 128), lambda bi, i: (bi, 0, i, 0)),
                  pl.BlockSpec((1, tq, 1), lambda bi, i: (bi, i, 0)),
                  pl.BlockSpec((1, nkt, KEY_TILE), lambda bi, i: (bi, 0, 0)),
                  full(128), full(128), full(128),
                  pl.BlockSpec((KEY_TILE, KEY_TILE), lambda bi, i: (0, 0))],
        out_specs=pl.BlockSpec((1, tq, att), lambda bi, i: (bi, i, 0)),
        out_shape=jax.ShapeDtypeStruct((b, s, att), BF16),
        scratch_shapes=[pltpu.VMEM((ATT_HEADS * tq, 128), BF16),
                        pltpu.VMEM((IDX_HEADS * tq, 128), BF16),
                        pltpu.VMEM((IDX_HEADS * tq, 128), F32),
                        pltpu.VMEM((nkt, tq, KEY_TILE), I32),
                        pltpu.VMEM((ATT_HEADS * tq, 128), F32)],
        compiler_params=_cparams(("arbitrary", "arbitrary")),
        name="dsa_attention",
    )(zm, zm, zm, zs, tabs, positions.reshape(b, s, 1), positions.reshape(b, nkt, KEY_TILE),
      k_rot, v_b, ki2, upper)


def _even_layer(x, scale, shift, gate, g, w_in, w_out, conv_w, conv_vec, mu_rkv, mu_lora, vec,
                w_up, a_up, r_k, final_g):
    d = x.shape[2]
    n6 = 6 * d
    lora = 2 * LORA_DIM
    w_main = jnp.concatenate([w_in[:, :n6], w_in[:, n6 + lora:]], axis=1).astype(BF16)
    w_small = w_in[:, n6:n6 + lora].astype(BF16)
    zm, zs = _norm_proj(x, g, scale, shift, w_main, w_small, tm=min(PROJ_ROWS, x.shape[1]), tn=1024)
    ya = _conv_branch(zm, conv_w[:, 0, :], conv_vec, cols=(0, 1, 2), t=min(CONV_ROWS, x.shape[1]))
    yb = _rwkv_branch(zm, zs, mu_rkv, mu_lora, vec, w_up, a_up, r_k, cols=(3, 4, 5, 6))
    wo = w_out.astype(BF16)
    return _out_proj([ya, yb], [wo[:d], wo[d:]], x, gate, final_g, tm=min(OUT_ROWS, x.shape[1]))


def _odd_layer(x, scale, shift, gate, g, w_in, w_out, positions, final_g):
    att = ATT_HEADS * ATT_HEAD_DIM
    idx = IDX_HEADS * IDX_HEAD_DIM
    o_q, o_k, o_v = 0, att, att + 128
    o_qi = o_v + 128
    o_ki = o_qi + idx
    o_wi = o_ki + IDX_HEAD_DIM
    o_g = o_wi + IDX_HEADS
    w_main = jnp.concatenate([w_in[:, o_q:o_k], w_in[:, o_g:], w_in[:, o_qi:o_ki]], axis=1).astype(BF16)
    pad = jnp.zeros((w_in.shape[0], 128 - IDX_HEAD_DIM - IDX_HEADS), w_in.dtype)
    w_small = jnp.concatenate([w_in[:, o_k:o_qi], w_in[:, o_ki:o_g], pad], axis=1).astype(BF16)
    zm, zs = _norm_proj(x, g, scale, shift, w_main, w_small, tm=min(PROJ_ROWS, x.shape[1]), tn=1536)
    k_rot, v_b, ki2, tabs = _rope_kv(zs, positions, t=min(ROPE_ROWS, x.shape[1]))
    o = _dsa(zm, zs, tabs, positions, k_rot, v_b, ki2)
    return _out_proj([o], [w_out.astype(BF16)], x, gate, final_g, tm=min(OUT_ROWS, x.shape[1]))


def kernel(x, c, positions, ada_w, ada_b, norm_g, final_g, even_w_in, even_w_out, conv_w, conv_vec,
           rwkv_mu_rkv, rwkv_mu_lora, rwkv_vec, rwkv_w_up, rwkv_a_up, rwkv_r_k, odd_w_in, odd_w_out):
    depth = ada_w.shape[0]
    d = x.shape[2]
    mod = _modulation(c, ada_w, ada_b)
    for l in range(depth):
        shift, scale, gate = mod[l, :, :d], mod[l, :, d:2 * d], mod[l, :, 2 * d:]
        fg = final_g if l == depth - 1 else None
        j = l // 2
        if l % 2 == 0:
            x = _even_layer(x, scale, shift, gate, norm_g[l], even_w_in[j], even_w_out[j], conv_w[j],
                            conv_vec[j], rwkv_mu_rkv[j], rwkv_mu_lora[j], rwkv_vec[j], rwkv_w_up[j],
                            rwkv_a_up[j], rwkv_r_k[j], fg)
        else:
            x = _odd_layer(x, scale, shift, gate, norm_g[l], odd_w_in[j], odd_w_out[j], positions, fg)
    return x
```

```python
import functools
import math

import jax
import jax.numpy as jnp
import numpy as np
from jax import lax
from jax.experimental import pallas as pl
from jax.experimental.pallas import tpu as pltpu

F32 = jnp.float32
BF16 = jnp.bfloat16
I32 = jnp.int32
I16 = jnp.int16

CHUNK = 64
EPS = 1e-6
LN_EPS = 1e-5
ROPE_THETA = 10000.0
CONV_WIDTH = 31
CONV_HALO = 32
RWKV_HEAD = 64
LORA_DIM = 64
GN_EPS = 64e-5
ATT_HEADS = 16
ATT_HEAD_DIM = 128
IDX_HEADS = 8
IDX_HEAD_DIM = 64
TOPK_MAX = 256
Q_BLOCK = 128
KEY_TILE = 256
QUAD = 256
HEADS_PER_QUAD = QUAD // RWKV_HEAD
RWKV_CHUNK = QUAD // HEADS_PER_QUAD
NEG_BIG = -1e30
INT_MIN = -(2 ** 31)
MIN16 = -(2 ** 15)
VMEM_LIMIT = 56 * 1024 * 1024
PROJ_ROWS = 1024
OUT_ROWS = 512
CONV_ROWS = 256
ROPE_ROWS = 512


def _cparams(sem):
    return pltpu.CompilerParams(dimension_semantics=sem, vmem_limit_bytes=VMEM_LIMIT)


def _split_bf16(x, n):
    if x.dtype == BF16:
        return [x]
    parts = []
    rem = x
    for i in range(n):
        p = rem.astype(BF16)
        parts.append(p)
        if i + 1 < n:
            rem = rem - p.astype(F32)
    return parts


def _mm(a, b, pa=1, pb=1, trans_b=False):
    ap = _split_bf16(a, pa)
    bp = _split_bf16(b, pb)
    dn = (((1,), (1 if trans_b else 0,)), ((), ()))
    order = max(len(ap), len(bp))
    acc = None
    for i, x in enumerate(ap):
        for j, y in enumerate(bp):
            if i + j >= order:
                continue
            t = lax.dot_general(x, y, dn, preferred_element_type=F32)
            acc = t if acc is None else acc + t
    return acc


def _sigmoid(x):
    return 1.0 / (1.0 + jnp.exp(-x))


def _silu(x):
    return x * _sigmoid(x)


def _mod_kernel(c_ref, w_ref, b_ref, o_ref):
    cond = _silu(c_ref[...])
    o_ref[0] = _mm(cond, w_ref[0], 2, 2) + b_ref[0]


def _modulation(c, ada_w, ada_b):
    depth, d, n = ada_w.shape
    b = c.shape[0]
    tn = 1024
    return pl.pallas_call(
        _mod_kernel,
        grid=(depth, n // tn),
        in_specs=[pl.BlockSpec((b, d), lambda l, j: (0, 0)),
                  pl.BlockSpec((1, d, tn), lambda l, j: (l, 0, j)),
                  pl.BlockSpec((1, 1, tn), lambda l, j: (l, 0, j))],
        out_specs=pl.BlockSpec((1, b, tn), lambda l, j: (l, 0, j)),
        out_shape=jax.ShapeDtypeStruct((depth, b, n), F32),
        compiler_params=_cparams(("arbitrary", "arbitrary")),
        name="adaln_mod",
    )(c, ada_w, ada_b.reshape(depth, 1, n))


def _norm_proj_kernel(x_ref, g_ref, sc_ref, sh_ref, wm_ref, ws_ref, zm_ref, zs_ref, h_scr):
    @pl.when(pl.program_id(2) == 0)
    def _():
        x = x_ref[0]
        ms = jnp.mean(x * x, axis=-1, keepdims=True)
        y = x * lax.rsqrt(ms + EPS) * g_ref[...]
        h = (y * (1.0 + sc_ref[0]) + sh_ref[0]).astype(BF16)
        h_scr[...] = h
        zs_ref[0] = jnp.dot(h, ws_ref[...], preferred_element_type=F32)

    zm_ref[0] = jnp.dot(h_scr[...], wm_ref[...], preferred_element_type=F32)


def _norm_proj(x, g, scale, shift, w_main, w_small, tm, tn):
    b, s, d = x.shape
    nm = w_main.shape[1]
    ns = w_small.shape[1]
    return pl.pallas_call(
        _norm_proj_kernel,
        grid=(b, s // tm, nm // tn),
        in_specs=[pl.BlockSpec((1, tm, d), lambda bi, i, j: (bi, i, 0)),
                  pl.BlockSpec((1, d), lambda bi, i, j: (0, 0)),
                  pl.BlockSpec((1, 1, d), lambda bi, i, j: (bi, 0, 0)),
                  pl.BlockSpec((1, 1, d), lambda bi, i, j: (bi, 0, 0)),
                  pl.BlockSpec((d, tn), lambda bi, i, j: (0, j)),
                  pl.BlockSpec((d, ns), lambda bi, i, j: (0, 0))],
        out_specs=[pl.BlockSpec((1, tm, tn), lambda bi, i, j: (bi, i, j)),
                   pl.BlockSpec((1, tm, ns), lambda bi, i, j: (bi, i, 0))],
        out_shape=[jax.ShapeDtypeStruct((b, s, nm), F32),
                   jax.ShapeDtypeStruct((b, s, ns), F32)],
        scratch_shapes=[pltpu.VMEM((tm, d), BF16)],
        compiler_params=_cparams(("arbitrary", "arbitrary", "arbitrary")),
        name="norm_proj",
    )(x, g.reshape(1, d), scale.reshape(b, 1, d), shift.reshape(b, 1, d), w_main, w_small)


def _conv_kernel(val_ref, glu_ref, gate_ref, w_ref, vec_ref, o_ref, u_scr):
    t = val_ref.shape[1]

    @pl.when(pl.program_id(1) == 0)
    def _():
        u_scr[0:CONV_HALO, :] = jnp.zeros((CONV_HALO, u_scr.shape[1]), F32)

    @pl.when(pl.program_id(1) > 0)
    def _():
        u_scr[0:CONV_HALO, :] = u_scr[t:t + CONV_HALO, :]

    u_scr[CONV_HALO:CONV_HALO + t, :] = val_ref[0] * _sigmoid(glu_ref[0])
    first = CONV_HALO - (CONV_WIDTH - 1)
    acc = w_ref[0:1, :] * u_scr[pl.ds(first, t), :]
    for j in range(1, CONV_WIDTH):
        acc = acc + w_ref[j:j + 1, :] * u_scr[pl.ds(first + j, t), :]
    y = acc + vec_ref[0:1, :]
    mu = jnp.mean(y, axis=-1, keepdims=True)
    yc = y - mu
    var = jnp.mean(yc * yc, axis=-1, keepdims=True)
    yn = yc * lax.rsqrt(var + LN_EPS) * vec_ref[1:2, :] + vec_ref[2:3, :]
    o_ref[0] = (_silu(yn) * _silu(gate_ref[0])).astype(o_ref.dtype)


def _conv_branch(zm, conv_w, conv_vec, cols, t):
    b, s, _ = zm.shape
    c = conv_w.shape[1]
    cv, cg, cs = cols
    return pl.pallas_call(
        _conv_kernel,
        grid=(b, s // t),
        in_specs=[pl.BlockSpec((1, t, c), lambda bi, i: (bi, i, cv)),
                  pl.BlockSpec((1, t, c), lambda bi, i: (bi, i, cg)),
                  pl.BlockSpec((1, t, c), lambda bi, i: (bi, i, cs)),
                  pl.BlockSpec((CONV_WIDTH, c), lambda bi, i: (0, 0)),
                  pl.BlockSpec((3, c), lambda bi, i: (0, 0))],
        out_specs=pl.BlockSpec((1, t, c), lambda bi, i: (bi, i, 0)),
        out_shape=jax.ShapeDtypeStruct((b, s, c), BF16),
        scratch_shapes=[pltpu.VMEM((CONV_HALO + t, c), F32)],
        compiler_params=_cparams(("arbitrary", "arbitrary")),
        name="conformer_conv",
    )(zm, zm, zm, conv_w, conv_vec)


def _token_shift(x, carry_row):
    rolled = pltpu.roll(x, 1, axis=0)
    row = lax.broadcasted_iota(I32, x.shape, 0)
    return jnp.where(row == 0, carry_row, rolled)


def _rwkv_kernel(r_ref, k_ref, v_ref, g_ref, lo_ref, mu_ref, mul_ref, vec_ref, rk_ref, wcat_ref,
                 tri_ref, ones_ref, o_ref, carry_scr, carryl_scr, h_scr, y_scr):
    c = r_ref.shape[1]
    dim = r_ref.shape[2]
    nquad = dim // QUAD

    @pl.when(pl.program_id(1) == 0)
    def _():
        carry_scr[...] = jnp.zeros(carry_scr.shape, F32)
        carryl_scr[...] = jnp.zeros(carryl_scr.shape, F32)
        h_scr[...] = jnp.zeros(h_scr.shape, F32)

    def lerp(idx, ref, mu_row):
        x = ref[0]
        prev = _token_shift(x, carry_scr[idx, 0:1, :])
        carry_scr[idx, 0:1, :] = x[c - 1:c, :]
        return x + (prev - x) * mu_row

    r = lerp(0, r_ref, mu_ref[0:1, :])
    k = lerp(1, k_ref, mu_ref[1:2, :])
    v = lerp(2, v_ref, mu_ref[2:3, :])
    lo0 = lo_ref[0]
    lo_prev = _token_shift(lo0, carryl_scr[0:1, :])
    carryl_scr[0:1, :] = lo0[c - 1:c, :]
    lo = lo0 + (lo_prev - lo0) * mul_ref[...]
    lane = lax.broadcasted_iota(I32, lo.shape, 1)
    lo_act = jnp.where(lane < LORA_DIM, jnp.tanh(lo), lo)
    pre = _mm(lo_act, wcat_ref[...], 2, 2)
    w_pre = pre[:, :dim] + vec_ref[0:1, :]
    a_pre = pre[:, dim:] + vec_ref[1:2, :]
    sp = jnp.maximum(-w_pre, 0.0) + jnp.log(1.0 + jnp.exp(-jnp.abs(w_pre)))
    logw = -jnp.exp(-sp - 0.5)
    a = _sigmoid(a_pre)

    ones_bd = ones_ref[...]

    def head_sum(x):
        return jnp.concatenate(
            [_mm(x[:, q * QUAD:(q + 1) * QUAD], ones_bd, 2, 1) for q in range(nquad)], axis=1)

    kk = k * vec_ref[2:3, :]
    kk = kk / jnp.maximum(jnp.sqrt(head_sum(kk * kk)), 1e-12)
    kmod = k * (1.0 + (a - 1.0) * vec_ref[3:4, :])
    kka = kk * a

    cum = _mm(tri_ref[...], logw, 1, 3)
    tot = cum[c - 1:c, :]
    e_neg = jnp.exp(-cum)
    e_rem = jnp.exp(tot - cum)
    at = -kk * jnp.exp(cum - logw)
    bt = kka * e_neg
    kt = kmod * e_neg
    rt = r * jnp.exp(cum)
    bh = kka * e_rem
    kh = kmod * e_rem
    gam = jnp.exp(tot)

    lane_h = lax.broadcasted_iota(I32, (c, QUAD), 1) // RWKV_HEAD
    n4 = HEADS_PER_QUAD * c
    row = lax.broadcasted_iota(I32, (n4, n4), 0)
    col = lax.broadcasted_iota(I32, (n4, n4), 1)
    strict = row > col
    incl = row >= col
    diag = row == col

    def stack(z):
        return jnp.concatenate([jnp.where(lane_h == h, z, 0.0) for h in range(HEADS_PER_QUAD)], axis=0)

    def unstack(zs):
        out = zs[0:c]
        for h in range(1, HEADS_PER_QUAD):
            out = out + zs[h * c:(h + 1) * c]
        return out

    for q in range(nquad):
        sl = slice(q * QUAD, (q + 1) * QUAD)
        sa = stack(at[:, sl])
        sv = stack(v[:, sl])
        lhs = jnp.concatenate([sa, stack(rt[:, sl])], axis=0)
        rhs = jnp.concatenate([stack(bt[:, sl]), stack(kt[:, sl])], axis=0)
        aa = _mm(lhs, rhs, 1, 1, trans_b=True)
        n_ab = jnp.where(strict, aa[0:n4, 0:n4], 0.0)
        a_ak = jnp.where(strict, aa[0:n4, n4:], 0.0)
        a_rb = jnp.where(incl, aa[n4:, 0:n4], 0.0)
        a_rk = jnp.where(incl, aa[n4:, n4:], 0.0)
        tm = jnp.where(diag, 1.0, n_ab)
        p = n_ab
        steps = int(math.log2(c)) - 1
        for _ in range(steps):
            p = _mm(p, p)
            tm = tm + _mm(tm, p)
        ta = _mm(tm, sa)
        tav = _mm(tm, _mm(a_ak, sv))
        bs_t = stack(bh[:, sl]).T
        ks_t = stack(kh[:, sl]).T
        m_low = _mm(bs_t, ta)
        g_new = _mm(bs_t, tav) + _mm(ks_t, sv)
        qmat = rt[:, sl] + unstack(_mm(a_rb, ta))
        y_in = unstack(_mm(a_rb, tav) + _mm(a_rk, sv))
        h0 = h_scr[q]
        gam_col = jnp.sum(jnp.where(diag, gam[:, sl], 0.0), axis=1, keepdims=True)
        y_scr[:, sl] = y_in + _mm(qmat, h0)
        h_scr[q] = gam_col * h0 + _mm(m_low, h0) + g_new

    y = y_scr[...]
    inv_n = 1.0 / RWKV_HEAD
    mu = head_sum(y) * inv_n
    yc = y - mu
    var = head_sum(yc * yc) * inv_n
    yn = yc * lax.rsqrt(var + GN_EPS) * vec_ref[4:5, :] + vec_ref[5:6, :]
    bonus = head_sum(r * kmod * rk_ref[...]) * v
    o_ref[0] = ((yn + bonus) * _silu(g_ref[0])).astype(o_ref.dtype)


def _rwkv_branch(zm, zs, mu_rkv, mu_lora, vec, w_up, a_up, r_k, cols):
    b, s, _ = zm.shape
    dim = mu_rkv.shape[1]
    c = RWKV_CHUNK
    cr, ck, cv, cg = cols
    zero = jnp.zeros((LORA_DIM, dim), F32)
    wcat = jnp.concatenate([jnp.concatenate([w_up, zero], axis=1),
                            jnp.concatenate([zero, a_up], axis=1)], axis=0)
    tri = jnp.asarray(np.tril(np.ones((c, c), np.float32)), BF16)
    hid = np.arange(QUAD) // RWKV_HEAD
    ones_bd = jnp.asarray((hid[:, None] == hid[None, :]).astype(np.float32), BF16)
    const = lambda shape: pl.BlockSpec(shape, lambda bi, i: (0,) * len(shape))
    return pl.pallas_call(
        _rwkv_kernel,
        grid=(b, s // c),
        in_specs=[pl.BlockSpec((1, c, dim), lambda bi, i: (bi, i, cr)),
                  pl.BlockSpec((1, c, dim), lambda bi, i: (bi, i, ck)),
                  pl.BlockSpec((1, c, dim), lambda bi, i: (bi, i, cv)),
                  pl.BlockSpec((1, c, dim), lambda bi, i: (bi, i, cg)),
                  pl.BlockSpec((1, c, 2 * LORA_DIM), lambda bi, i: (bi, i, 0)),
                  const((3, dim)), const((1, 2 * LORA_DIM)), const((6, dim)), const((1, dim)),
                  const((2 * LORA_DIM, 2 * dim)), const((c, c)), const((QUAD, QUAD))],
        out_specs=pl.BlockSpec((1, c, dim), lambda bi, i: (bi, i, 0)),
        out_shape=jax.ShapeDtypeStruct((b, s, dim), BF16),
        scratch_shapes=[pltpu.VMEM((3, 8, dim), F32),
                        pltpu.VMEM((8, 2 * LORA_DIM), F32),
                        pltpu.VMEM((dim // QUAD, QUAD, QUAD), F32),
                        pltpu.VMEM((c, dim), F32)],
        compiler_params=_cparams(("arbitrary", "arbitrary")),
        name="rwkv7_chunk",
    )(zm, zm, zm, zm, zs, mu_rkv, mu_lora.reshape(1, 2 * LORA_DIM), vec, r_k.reshape(1, dim),
      wcat, tri, ones_bd)


def _out_proj_kernel(*refs, n_in, final):
    y_refs = refs[:n_in]
    w_refs = refs[n_in:2 * n_in]
    x_ref, gate_ref = refs[2 * n_in], refs[2 * n_in + 1]
    o_ref = refs[-1]
    acc = jnp.dot(y_refs[0][0], w_refs[0][...], preferred_element_type=F32)
    for y_ref, w_ref in zip(y_refs[1:], w_refs[1:]):
        acc = acc + jnp.dot(y_ref[0], w_ref[...], preferred_element_type=F32)
    xn = x_ref[0] + gate_ref[0] * acc
    if final:
        fg_ref = refs[2 * n_in + 2]
        ms = jnp.mean(xn * xn, axis=-1, keepdims=True)
        xn = xn * lax.rsqrt(ms + EPS) * fg_ref[...]
    o_ref[0] = xn


def _out_proj(ys, ws, x, gate, final_g, tm):
    b, s, d = x.shape
    n_in = len(ys)
    final = final_g is not None
    in_specs = [pl.BlockSpec((1, tm, y.shape[2]), lambda bi, i: (bi, i, 0)) for y in ys]
    in_specs += [pl.BlockSpec(w.shape, lambda bi, i: (0, 0)) for w in ws]
    in_specs += [pl.BlockSpec((1, tm, d), lambda bi, i: (bi, i, 0)),
                 pl.BlockSpec((1, 1, d), lambda bi, i: (bi, 0, 0))]
    args = list(ys) + list(ws) + [x, gate.reshape(b, 1, d)]
    if final:
        in_specs.append(pl.BlockSpec((1, d), lambda bi, i: (0, 0)))
        args.append(final_g.reshape(1, d))
    return pl.pallas_call(
        functools.partial(_out_proj_kernel, n_in=n_in, final=final),
        grid=(b, s // tm),
        in_specs=in_specs,
        out_specs=pl.BlockSpec((1, tm, d), lambda bi, i: (bi, i, 0)),
        out_shape=jax.ShapeDtypeStruct((b, s, d), F32),
        compiler_params=_cparams(("arbitrary", "arbitrary")),
        name="out_proj_final" if final else "out_proj",
    )(*args)


def _rope128(x, cos, sin_signed):
    return x * cos + pltpu.roll(x, 64, axis=1) * sin_signed


def _rope64_pair(x, cos, sin_signed):
    lane = lax.broadcasted_iota(I32, x.shape, 1)
    first_half = (lane % IDX_HEAD_DIM) < (IDX_HEAD_DIM // 2)
    partner = jnp.where(first_half, pltpu.roll(x, 96, axis=1), pltpu.roll(x, 32, axis=1))
    return x * cos + partner * sin_signed


def _rope_kv_kernel(zs_ref, pos_ref, inv_ref, sgn_ref, k_ref, v_ref, ki_ref, tab_ref):
    posf = pos_ref[0].astype(F32)
    ang128 = posf * inv_ref[0:1, :]
    ang64 = posf * inv_ref[1:2, :]
    c128 = jnp.cos(ang128)
    s128 = jnp.sin(ang128) * sgn_ref[0:1, :]
    c64 = jnp.cos(ang64)
    s64 = jnp.sin(ang64) * sgn_ref[1:2, :]
    tab_ref[0, 0] = c128
    tab_ref[0, 1] = s128
    tab_ref[0, 2] = c64
    tab_ref[0, 3] = s64
    zs = zs_ref[0]
    k_ref[0] = _rope128(zs[:, 0:128], c128, s128).astype(BF16)
    v_ref[0] = zs[:, 128:256].astype(BF16)
    ki = _rope64_pair(zs[:, 256:384], c64, s64)
    lane = lax.broadcasted_iota(I32, ki.shape, 1)
    ki_ref[0] = jnp.where(lane < IDX_HEAD_DIM, ki, pltpu.roll(ki, 64, axis=1)).astype(BF16)


def _rope_kv(zs, positions, t):
    b, s, ns = zs.shape
    inv128 = ROPE_THETA ** (-jnp.arange(0, ATT_HEAD_DIM, 2, dtype=F32) / ATT_HEAD_DIM)
    inv64 = ROPE_THETA ** (-jnp.arange(0, IDX_HEAD_DIM, 2, dtype=F32) / IDX_HEAD_DIM)
    inv = jnp.stack([jnp.tile(inv128, 2), jnp.tile(inv64, 4)])
    sgn128 = np.where(np.arange(128) < 64, -1.0, 1.0)
    sgn64 = np.where((np.arange(128) % 64) < 32, -1.0, 1.0)
    sgn = jnp.asarray(np.stack([sgn128, sgn64]), F32)
    blk = lambda w: pl.BlockSpec((1, t, w), lambda bi, i: (bi, i, 0))
    return pl.pallas_call(
        _rope_kv_kernel,
        grid=(b, s // t),
        in_specs=[blk(ns), blk(1),
                  pl.BlockSpec((2, 128), lambda bi, i: (0, 0)),
                  pl.BlockSpec((2, 128), lambda bi, i: (0, 0))],
        out_specs=[blk(128), blk(128), blk(128),
                   pl.BlockSpec((1, 4, t, 128), lambda bi, i: (bi, 0, i, 0))],
        out_shape=[jax.ShapeDtypeStruct((b, s, 128), BF16),
                   jax.ShapeDtypeStruct((b, s, 128), BF16),
                   jax.ShapeDtypeStruct((b, s, 128), BF16),
                   jax.ShapeDtypeStruct((b, 4, s, 128), F32)],
        compiler_params=_cparams(("arbitrary", "arbitrary")),
        name="rope_kv",
    )(zs, positions.reshape(b, s, 1), inv, sgn)


def _dsa_kernel(q_ref, g_ref, qi_ref, zs_ref, tab_ref, qpos_ref, kpos_ref, k_ref, v_ref, ki_ref,
                upper_ref, o_ref, qs_scr, qis_scr, wb_scr, key_scr, hi_scr, lo_scr, s_scr, p_scr,
                acc_scr, m_scr, l_scr, a_scr, *, k_sel):
    tq = q_ref.shape[1]
    qb = pl.program_id(1)
    nk = (qb * tq + tq + KEY_TILE - 1) // KEY_TILE
    npair = (nk + 1) // 2

    c128, s128 = tab_ref[0, 0], tab_ref[0, 1]
    c64, s64 = tab_ref[0, 2], tab_ref[0, 3]
    q_scale = ATT_HEAD_DIM ** -0.5
    for h in range(ATT_HEADS):
        qh = _rope128(q_ref[0, :, h * 128:(h + 1) * 128], c128, s128) * q_scale
        qs_scr[h * tq:(h + 1) * tq, :] = qh.astype(BF16)
    lane = lax.broadcasted_iota(I32, (tq, 128), 1)
    zs = zs_ref[0]
    w_scale = (IDX_HEADS ** -0.5) * (IDX_HEAD_DIM ** -0.5)
    w_off = 256 + IDX_HEAD_DIM
    for p in range(IDX_HEADS // 2):
        pair = _rope64_pair(qi_ref[0, :, p * 128:(p + 1) * 128], c64, s64)
        qis_scr[(2 * p) * tq:(2 * p + 1) * tq, :] = jnp.where(lane < 64, pair, 0.0).astype(BF16)
        qis_scr[(2 * p + 1) * tq:(2 * p + 2) * tq, :] = jnp.where(lane >= 64, pair, 0.0).astype(BF16)
    for h in range(IDX_HEADS):
        wcol = zs[:, w_off + h:w_off + h + 1] * w_scale
        wb_scr[h * tq:(h + 1) * tq, :] = jnp.broadcast_to(wcol, (tq, 128))

    qchunk = qpos_ref[0] >> 6

    def score_body(j, carry):
        start = pl.multiple_of(j * KEY_TILE, KEY_TILE)
        kt = ki_ref[0, pl.ds(start, KEY_TILE), :]
        sc = _mm(qis_scr[...], kt, trans_b=True)
        sc = jnp.maximum(sc, 0.0) * jnp.concatenate([wb_scr[...]] * (KEY_TILE // 128), axis=1)
        tot = sc[0:tq]
        for h in range(1, IDX_HEADS):
            tot = tot + sc[h * tq:(h + 1) * tq]
        kchunk = kpos_ref[0, pl.ds(j, 1), :] >> 6
        tot = jnp.where(kchunk <= qchunk, tot, -jnp.inf)
        bits = pltpu.bitcast(tot, I32)
        key = bits ^ ((bits >> 31) & 0x7FFFFFFF)
        key_scr[j] = key
        hi_scr[j] = (key >> 16).astype(I16)
        return carry

    lax.fori_loop(0, nk, score_body, 0)
    neg_key = INT_MIN + 0x7FFFFF

    @pl.when(nk % 2 == 1)
    def _():
        key_scr[nk] = jnp.full((tq, KEY_TILE), INT_MIN, I32)
        hi_scr[nk] = jnp.full((tq, KEY_TILE), MIN16, I16)
        lo_scr[nk] = jnp.full((tq, KEY_TILE), MIN16, I16)

    one16 = jnp.ones((tq, KEY_TILE), I16)
    zero16 = jnp.zeros((tq, KEY_TILE), I16)

    def to16(col):
        return jnp.broadcast_to(col, (tq, KEY_TILE)).astype(I16)

    def count16(plane, cand16, strict):
        cmp = (lambda x: x > cand16) if strict else (lambda x: x >= cand16)

        def body(jp, acc):
            acc = acc + jnp.where(cmp(plane[2 * jp]), one16, zero16)
            return acc + jnp.where(cmp(plane[2 * jp + 1]), one16, zero16)

        acc = lax.fori_loop(0, npair, body, zero16)
        return jnp.sum(acc.astype(I32), axis=1, keepdims=True)

    def kth_largest16(plane, k_need):
        def bit_body(i, thr):
            cand = thr + jnp.left_shift(jnp.int32(1), 15 - i)
            cnt = count16(plane, to16(cand), False)
            return jnp.where(cnt >= k_need, cand, thr)
        return lax.fori_loop(0, 16, bit_body, jnp.full((tq, 1), MIN16, I32))

    thi = kth_largest16(hi_scr, k_sel)
    n_above = count16(hi_scr, to16(thi), True)

    def lo_body(j, carry):
        key = key_scr[j]
        lo = (key & 0xFFFF) - 32768
        lo_scr[j] = jnp.where((key >> 16) == thi, lo, MIN16).astype(I16)
        return carry

    lax.fori_loop(0, nk, lo_body, 0)
    tlo = kth_largest16(lo_scr, k_sel - n_above)
    n_gt = n_above + count16(lo_scr, to16(tlo), True)
    thr = thi * 65536 + (tlo + 32768)
    need = (k_sel - n_gt).astype(F32)

    acc_scr[...] = jnp.zeros(acc_scr.shape, F32)
    l_scr[...] = jnp.zeros(l_scr.shape, F32)
    m_scr[...] = jnp.full(m_scr.shape, NEG_BIG, F32)
    upper = upper_ref[...]
    att_tile = 2 * KEY_TILE

    def att_body(jp, eqc):
        parts = []
        for t in range(2):
            key = key_scr[2 * jp + t]
            eqf = jnp.where(key == thr, 1.0, 0.0)
            before = _mm(eqf.astype(BF16), upper) + eqc
            take = jnp.where(key > thr, 1.0, jnp.where(before < need, eqf, 0.0))
            parts.append(jnp.where((take > 0.0) & (key > neg_key), 0.0, NEG_BIG))
            eqc = eqc + jnp.sum(eqf, axis=1, keepdims=True)
        bias = jnp.concatenate(parts, axis=1)
        start = pl.multiple_of(jp * att_tile, att_tile)
        s_scr[...] = _mm(qs_scr[...], k_ref[0, pl.ds(start, att_tile), :], trans_b=True)
        for h in range(ATT_HEADS):
            rs = slice(h * tq, (h + 1) * tq)
            s = s_scr[rs, :] + bias
            m_old = m_scr[rs, :]
            m_new = jnp.maximum(m_old, jnp.max(s, axis=1, keepdims=True))
            alpha = jnp.exp(m_old - m_new)
            p = jnp.exp(s - jnp.concatenate([m_new] * (att_tile // 128), axis=1))
            l_scr[rs, :] = alpha * l_scr[rs, :] + jnp.sum(p, axis=1, keepdims=True)
            m_scr[rs, :] = m_new
            a_scr[rs, :] = alpha
            p_scr[rs, :] = p.astype(BF16)
        pv = _mm(p_scr[...], v_ref[0, pl.ds(start, att_tile), :])
        acc_scr[...] = a_scr[...] * acc_scr[...] + pv
        return eqc

    lax.fori_loop(0, npair, att_body, jnp.zeros((tq, 1), F32))
    o = acc_scr[...] / l_scr[...]
    for h in range(ATT_HEADS):
        sl = slice(h * 128, (h + 1) * 128)
        o_ref[0, :, sl] = (o[h * tq:(h + 1) * tq, :] * _silu(g_ref[0, :, sl])).astype(o_ref.dtype)


def _dsa(zm, zs, tabs, positions, k_rot, v_b, ki2):
    b, s, _ = zm.shape
    tq = Q_BLOCK
    att = ATT_HEADS * ATT_HEAD_DIM
    idx = IDX_HEADS * IDX_HEAD_DIM
    k_sel = min(TOPK_MAX, s // 4)
    nkt = s // KEY_TILE
    assert nkt % 2 == 0, "key tiles are consumed in pairs"
    rows = ATT_HEADS * tq
    upper = jnp.asarray(np.triu(np.ones((KEY_TILE, KEY_TILE), np.float32), 1), BF16)
    full = lambda w: pl.BlockSpec((1, s, w), lambda bi, i: (bi, 0, 0))
    return pl.pallas_call(
        functools.partial(_dsa_kernel, k_sel=k_sel),
        grid=(b, s // tq),
        in_specs=[pl.BlockSpec((1, tq, att), lambda bi, i: (bi, i, 0)),
                  pl.BlockSpec((1, tq, att), lambda bi, i: (bi, i, 1)),
                  pl.BlockSpec((1, tq, idx), lambda bi, i: (bi, i, 2 * att // idx)),
                  pl.BlockSpec((1, tq, zs.shape[2]), lambda bi, i: (bi, i, 0)),
                  pl.BlockSpec((1, 4, tq, 128), lambda bi, i: (bi, 0, i, 0)),
                  pl.BlockSpec((1, tq, 1), lambda bi, i: (bi, i, 0)),
                  pl.BlockSpec((1, nkt, KEY_TILE), lambda bi, i: (bi, 0, 0)),
                  full(128), full(128), full(128),
                  pl.BlockSpec((KEY_TILE, KEY_TILE), lambda bi, i: (0, 0))],
        out_specs=pl.BlockSpec((1, tq, att), lambda bi, i: (bi, i, 0)),
        out_shape=jax.ShapeDtypeStruct((b, s, att), BF16),
        scratch_shapes=[pltpu.VMEM((rows, 128), BF16),
                        pltpu.VMEM((IDX_HEADS * tq, 128), BF16),
                        pltpu.VMEM((IDX_HEADS * tq, 128), F32),
                        pltpu.VMEM((nkt, tq, KEY_TILE), I32),
                        pltpu.VMEM((nkt, tq, KEY_TILE), I16),
                        pltpu.VMEM((nkt, tq, KEY_TILE), I16),
                        pltpu.VMEM((rows, 2 * KEY_TILE), F32),
                        pltpu.VMEM((rows, 2 * KEY_TILE), BF16),
                        pltpu.VMEM((rows, 128), F32),
                        pltpu.VMEM((rows, 128), F32),
                        pltpu.VMEM((rows, 128), F32),
                        pltpu.VMEM((rows, 128), F32)],
        compiler_params=_cparams(("arbitrary", "arbitrary")),
        name="dsa_attention",
    )(zm, zm, zm, zs, tabs, positions.reshape(b, s, 1), positions.reshape(b, nkt, KEY_TILE),
      k_rot, v_b, ki2, upper)


def _even_layer(x, scale, shift, gate, g, w_in, w_out, conv_w, conv_vec, mu_rkv, mu_lora, vec,
                w_up, a_up, r_k, final_g):
    d = x.shape[2]
    n6 = 6 * d
    lora = 2 * LORA_DIM
    w_main = jnp.concatenate([w_in[:, :n6], w_in[:, n6 + lora:]], axis=1).astype(BF16)
    w_small = w_in[:, n6:n6 + lora].astype(BF16)
    zm, zs = _norm_proj(x, g, scale, shift, w_main, w_small, tm=min(PROJ_ROWS, x.shape[1]), tn=1024)
    ya = _conv_branch(zm, conv_w[:, 0, :], conv_vec, cols=(0, 1, 2), t=min(CONV_ROWS, x.shape[1]))
    yb = _rwkv_branch(zm, zs, mu_rkv, mu_lora, vec, w_up, a_up, r_k, cols=(3, 4, 5, 6))
    wo = w_out.astype(BF16)
    return _out_proj([ya, yb], [wo[:d], wo[d:]], x, gate, final_g, tm=min(OUT_ROWS, x.shape[1]))


def _odd_layer(x, scale, shift, gate, g, w_in, w_out, positions, final_g):
    att = ATT_HEADS * ATT_HEAD_DIM
    idx = IDX_HEADS * IDX_HEAD_DIM
    o_q, o_k, o_v = 0, att, att + 128
    o_qi = o_v + 128
    o_ki = o_qi + idx
    o_wi = o_ki + IDX_HEAD_DIM
    o_g = o_wi + IDX_HEADS
    w_main = jnp.concatenate([w_in[:, o_q:o_k], w_in[:, o_g:], w_in[:, o_qi:o_ki]], axis=1).astype(BF16)
    pad = jnp.zeros((w_in.shape[0], 128 - IDX_HEAD_DIM - IDX_HEADS), w_in.dtype)
    w_small = jnp.concatenate([w_in[:, o_k:o_qi], w_in[:, o_ki:o_g], pad], axis=1).astype(BF16)
    zm, zs = _norm_proj(x, g, scale, shift, w_main, w_small, tm=min(PROJ_ROWS, x.shape[1]), tn=1536)
    k_rot, v_b, ki2, tabs = _rope_kv(zs, positions, t=min(ROPE_ROWS, x.shape[1]))
    o = _dsa(zm, zs, tabs, positions, k_rot, v_b, ki2)
    return _out_proj([o], [w_out.astype(BF16)], x, gate, final_g, tm=min(OUT_ROWS, x.shape[1]))


def kernel(x, c, positions, ada_w, ada_b, norm_g, final_g, even_w_in, even_w_out, conv_w, conv_vec,
           rwkv_mu_rkv, rwkv_mu_lora, rwkv_vec, rwkv_w_up, rwkv_a_up, rwkv_r_k, odd_w_in, odd_w_out):
    depth = ada_w.shape[0]
    d = x.shape[2]
    mod = _modulation(c, ada_w, ada_b)
    for l in range(depth):
        shift, scale, gate = mod[l, :, :d], mod[l, :, d:2 * d], mod[l, :, 2 * d:]
        fg = final_g if l == depth - 1 else None
        j = l // 2
        if l % 2 == 0:
            x = _even_layer(x, scale, shift, gate, norm_g[l], even_w_in[j], even_w_out[j], conv_w[j],
                            conv_vec[j], rwkv_mu_rkv[j], rwkv_mu_lora[j], rwkv_vec[j], rwkv_w_up[j],
                            rwkv_a_up[j], rwkv_r_k[j], fg)
        else:
            x = _odd_layer(x, scale, shift, gate, norm_g[l], odd_w_in[j], odd_w_out[j], positions, fg)
    return x
```

```python
import functools
import math

import jax
import jax.numpy as jnp
import numpy as np
from jax import lax
from jax.experimental import pallas as pl
from jax.experimental.pallas import tpu as pltpu

F32 = jnp.float32
BF16 = jnp.bfloat16
I32 = jnp.int32
I16 = jnp.int16

CHUNK = 64
EPS = 1e-6
LN_EPS = 1e-5
ROPE_THETA = 10000.0
CONV_WIDTH = 31
CONV_HALO = 32
SUBLANES = 8
LANES = 128
CONV_ACC_ROWS = 128
RWKV_HEAD = 64
LORA_DIM = 64
GN_EPS = 64e-5
ATT_HEADS = 16
ATT_HEAD_DIM = 128
IDX_HEADS = 8
IDX_HEAD_DIM = 64
TOPK_MAX = 256
Q_BLOCK = 128
KEY_TILE = 256
QUAD = 256
HEADS_PER_QUAD = QUAD // RWKV_HEAD
RWKV_CHUNK = QUAD // HEADS_PER_QUAD
NEG_BIG = -1e30
INT_MIN = -(2 ** 31)
MIN16 = -(2 ** 15)
VMEM_LIMIT = 56 * 1024 * 1024
PROJ_ROWS = 1024
OUT_ROWS = 512
CONV_ROWS = 256


def _cparams(sem):
    return pltpu.CompilerParams(dimension_semantics=sem, vmem_limit_bytes=VMEM_LIMIT)


def _split_bf16(x, n):
    if x.dtype == BF16:
        return [x]
    parts = []
    rem = x
    for i in range(n):
        p = rem.astype(BF16)
        parts.append(p)
        if i + 1 < n:
            rem = rem - p.astype(F32)
    return parts


def _mm(a, b, pa=1, pb=1, trans_b=False):
    ap = _split_bf16(a, pa)
    bp = _split_bf16(b, pb)
    dn = (((1,), (1 if trans_b else 0,)), ((), ()))
    order = max(len(ap), len(bp))
    acc = None
    for i, x in enumerate(ap):
        for j, y in enumerate(bp):
            if i + j >= order:
                continue
            t = lax.dot_general(x, y, dn, preferred_element_type=F32)
            acc = t if acc is None else acc + t
    return acc


def _sigmoid(x):
    return 1.0 / (1.0 + jnp.exp(-x))


def _silu(x):
    return x * _sigmoid(x)


def _mod_kernel(c_ref, w_ref, b_ref, o_ref):
    cond = _silu(c_ref[...])
    o_ref[0] = _mm(cond, w_ref[0], 2, 2) + b_ref[0]


def _modulation(c, ada_w, ada_b):
    depth, d, n = ada_w.shape
    b = c.shape[0]
    tn = 1024
    return pl.pallas_call(
        _mod_kernel,
        grid=(depth, n // tn),
        in_specs=[pl.BlockSpec((b, d), lambda l, j: (0, 0)),
                  pl.BlockSpec((1, d, tn), lambda l, j: (l, 0, j)),
                  pl.BlockSpec((1, 1, tn), lambda l, j: (l, 0, j))],
        out_specs=pl.BlockSpec((1, b, tn), lambda l, j: (l, 0, j)),
        out_shape=jax.ShapeDtypeStruct((depth, b, n), F32),
        compiler_params=_cparams(("arbitrary", "arbitrary")),
        name="adaln_mod",
    )(c, ada_w, ada_b.reshape(depth, 1, n))


def _norm_proj_kernel(x_ref, g_ref, sc_ref, sh_ref, wm_ref, ws_ref, zm_ref, zs_ref, h_scr):
    @pl.when(pl.program_id(2) == 0)
    def _():
        x = x_ref[0]
        ms = jnp.mean(x * x, axis=-1, keepdims=True)
        y = x * lax.rsqrt(ms + EPS) * g_ref[...]
        h = (y * (1.0 + sc_ref[0]) + sh_ref[0]).astype(BF16)
        h_scr[...] = h
        zs_ref[0] = jnp.dot(h, ws_ref[...], preferred_element_type=F32)

    zm_ref[0] = jnp.dot(h_scr[...], wm_ref[...], preferred_element_type=F32)


def _norm_proj(x, g, scale, shift, w_main, w_small, tm, tn):
    b, s, d = x.shape
    nm = w_main.shape[1]
    ns = w_small.shape[1]
    return pl.pallas_call(
        _norm_proj_kernel,
        grid=(b, s // tm, nm // tn),
        in_specs=[pl.BlockSpec((1, tm, d), lambda bi, i, j: (bi, i, 0)),
                  pl.BlockSpec((1, d), lambda bi, i, j: (0, 0)),
                  pl.BlockSpec((1, 1, d), lambda bi, i, j: (bi, 0, 0)),
                  pl.BlockSpec((1, 1, d), lambda bi, i, j: (bi, 0, 0)),
                  pl.BlockSpec((d, tn), lambda bi, i, j: (0, j)),
                  pl.BlockSpec((d, ns), lambda bi, i, j: (0, 0))],
        out_specs=[pl.BlockSpec((1, tm, tn), lambda bi, i, j: (bi, i, j)),
                   pl.BlockSpec((1, tm, ns), lambda bi, i, j: (bi, i, 0))],
        out_shape=[jax.ShapeDtypeStruct((b, s, nm), F32),
                   jax.ShapeDtypeStruct((b, s, ns), F32)],
        scratch_shapes=[pltpu.VMEM((tm, d), BF16)],
        compiler_params=_cparams(("arbitrary", "arbitrary", "arbitrary")),
        name="norm_proj",
    )(x, g.reshape(1, d), scale.reshape(b, 1, d), shift.reshape(b, 1, d), w_main, w_small)


def _conv_kernel(val_ref, glu_ref, gate_ref, w_ref, vec_ref, o_ref, u_scr, sh_scr, y_scr):
    t = val_ref.shape[1]
    c = val_ref.shape[2]

    @pl.when(pl.program_id(1) == 0)
    def _():
        u_scr[0:CONV_HALO, :] = jnp.zeros((CONV_HALO, u_scr.shape[1]), F32)

    @pl.when(pl.program_id(1) > 0)
    def _():
        u_scr[0:CONV_HALO, :] = u_scr[t:t + CONV_HALO, :]

    u_scr[CONV_HALO:CONV_HALO + t, :] = val_ref[0] * _sigmoid(glu_ref[0])
    ext = sh_scr.shape[1]
    for a in range(1, SUBLANES):
        sh_scr[a - 1] = u_scr[pl.ds(a, ext), :]
    first = CONV_HALO - (CONV_WIDTH - 1)
    rb = CONV_ACC_ROWS
    for cs in range(c // LANES):
        ls = slice(cs * LANES, (cs + 1) * LANES)
        for r0 in range(0, t, rb):
            acc = None
            for j in range(CONV_WIDTH):
                a, k8 = (first + j) % SUBLANES, (first + j) // SUBLANES * SUBLANES
                rows = slice(k8 + r0, k8 + r0 + rb)
                blk = u_scr[rows, ls] if a == 0 else sh_scr[a - 1, rows, ls]
                term = w_ref[j:j + 1, ls] * blk
                acc = term if acc is None else acc + term
            y_scr[r0:r0 + rb, ls] = acc + vec_ref[0:1, ls]
    y = y_scr[...]
    mu = jnp.mean(y, axis=-1, keepdims=True)
    yc = y - mu
    var = jnp.mean(yc * yc, axis=-1, keepdims=True)
    yn = yc * lax.rsqrt(var + LN_EPS) * vec_ref[1:2, :] + vec_ref[2:3, :]
    o_ref[0] = (_silu(yn) * _silu(gate_ref[0])).astype(o_ref.dtype)


def _conv_branch(zm, conv_w, conv_vec, cols, t):
    b, s, _ = zm.shape
    c = conv_w.shape[1]
    cv, cg, cs = cols
    return pl.pallas_call(
        _conv_kernel,
        grid=(b, s // t),
        in_specs=[pl.BlockSpec((1, t, c), lambda bi, i: (bi, i, cv)),
                  pl.BlockSpec((1, t, c), lambda bi, i: (bi, i, cg)),
                  pl.BlockSpec((1, t, c), lambda bi, i: (bi, i, cs)),
                  pl.BlockSpec((CONV_WIDTH, c), lambda bi, i: (0, 0)),
                  pl.BlockSpec((3, c), lambda bi, i: (0, 0))],
        out_specs=pl.BlockSpec((1, t, c), lambda bi, i: (bi, i, 0)),
        out_shape=jax.ShapeDtypeStruct((b, s, c), BF16),
        scratch_shapes=[pltpu.VMEM((CONV_HALO + t, c), F32),
                        pltpu.VMEM((SUBLANES - 1, t + CONV_HALO - SUBLANES, c), F32),
                        pltpu.VMEM((t, c), F32)],
        compiler_params=_cparams(("arbitrary", "arbitrary")),
        name="conformer_conv",
    )(zm, zm, zm, conv_w, conv_vec)


def _token_shift(x, carry_row):
    rolled = pltpu.roll(x, 1, axis=0)
    row = lax.broadcasted_iota(I32, x.shape, 0)
    return jnp.where(row == 0, carry_row, rolled)


def _rwkv_kernel(r_ref, k_ref, v_ref, g_ref, lo_ref, mu_ref, mul_ref, vec_ref, rk_ref, wcat_ref,
                 tri_ref, ones_ref, o_ref, carry_scr, carryl_scr, h_scr, y_scr):
    c = r_ref.shape[1]
    dim = r_ref.shape[2]
    nquad = dim // QUAD

    @pl.when(pl.program_id(1) == 0)
    def _():
        carry_scr[...] = jnp.zeros(carry_scr.shape, F32)
        carryl_scr[...] = jnp.zeros(carryl_scr.shape, F32)
        h_scr[...] = jnp.zeros(h_scr.shape, F32)

    def lerp(idx, ref, mu_row):
        x = ref[0]
        prev = _token_shift(x, carry_scr[idx, 0:1, :])
        carry_scr[idx, 0:1, :] = x[c - 1:c, :]
        return x + (prev - x) * mu_row

    r = lerp(0, r_ref, mu_ref[0:1, :])
    k = lerp(1, k_ref, mu_ref[1:2, :])
    v = lerp(2, v_ref, mu_ref[2:3, :])
    lo0 = lo_ref[0]
    lo_prev = _token_shift(lo0, carryl_scr[0:1, :])
    carryl_scr[0:1, :] = lo0[c - 1:c, :]
    lo = lo0 + (lo_prev - lo0) * mul_ref[...]
    lane = lax.broadcasted_iota(I32, lo.shape, 1)
    lo_act = jnp.where(lane < LORA_DIM, jnp.tanh(lo), lo)
    pre = _mm(lo_act, wcat_ref[...], 2, 2)
    w_pre = pre[:, :dim] + vec_ref[0:1, :]
    a_pre = pre[:, dim:] + vec_ref[1:2, :]
    sp = jnp.maximum(-w_pre, 0.0) + jnp.log(1.0 + jnp.exp(-jnp.abs(w_pre)))
    logw = -jnp.exp(-sp - 0.5)
    a = _sigmoid(a_pre)

    ones_bd = ones_ref[...]

    def head_sum(x):
        return jnp.concatenate(
            [_mm(x[:, q * QUAD:(q + 1) * QUAD], ones_bd, 2, 1) for q in range(nquad)], axis=1)

    kk = k * vec_ref[2:3, :]
    kk = kk / jnp.maximum(jnp.sqrt(head_sum(kk * kk)), 1e-12)
    kmod = k * (1.0 + (a - 1.0) * vec_ref[3:4, :])
    kka = kk * a

    cum = _mm(tri_ref[...], logw, 1, 3)
    tot = cum[c - 1:c, :]
    e_neg = jnp.exp(-cum)
    e_rem = jnp.exp(tot - cum)
    at = -kk * jnp.exp(cum - logw)
    bt = kka * e_neg
    kt = kmod * e_neg
    rt = r * jnp.exp(cum)
    bh = kka * e_rem
    kh = kmod * e_rem
    gam = jnp.exp(tot)

    lane_h = lax.broadcasted_iota(I32, (c, QUAD), 1) // RWKV_HEAD
    n4 = HEADS_PER_QUAD * c
    row = lax.broadcasted_iota(I32, (n4, n4), 0)
    col = lax.broadcasted_iota(I32, (n4, n4), 1)
    strict = row > col
    incl = row >= col
    diag = row == col

    def stack(z):
        return jnp.concatenate([jnp.where(lane_h == h, z, 0.0) for h in range(HEADS_PER_QUAD)], axis=0)

    def unstack(zs):
        out = zs[0:c]
        for h in range(1, HEADS_PER_QUAD):
            out = out + zs[h * c:(h + 1) * c]
        return out

    qs_ = range(nquad)
    sls = [slice(q * QUAD, (q + 1) * QUAD) for q in qs_]
    sa = [stack(at[:, sl]) for sl in sls]
    sv = [stack(v[:, sl]) for sl in sls]
    aa = [_mm(jnp.concatenate([sa[q], stack(rt[:, sls[q]])], axis=0),
              jnp.concatenate([stack(bt[:, sls[q]]), stack(kt[:, sls[q]])], axis=0), trans_b=True)
          for q in qs_]
    n_ab = [jnp.where(strict, x[0:n4, 0:n4], 0.0) for x in aa]
    a_ak = [jnp.where(strict, x[0:n4, n4:], 0.0) for x in aa]
    a_rb = [jnp.where(incl, x[n4:, 0:n4], 0.0) for x in aa]
    a_rk = [jnp.where(incl, x[n4:, n4:], 0.0) for x in aa]
    tm = [jnp.where(diag, 1.0, x) for x in n_ab]
    p = n_ab
    for _ in range(int(math.log2(c)) - 1):
        p = [_mm(x, x) for x in p]
        tm = [t + _mm(t, x) for t, x in zip(tm, p)]
    ta = [_mm(tm[q], sa[q]) for q in qs_]
    av = [_mm(a_ak[q], sv[q]) for q in qs_]
    tav = [_mm(tm[q], av[q]) for q in qs_]
    bs_t = [stack(bh[:, sl]).T for sl in sls]
    ks_t = [stack(kh[:, sl]).T for sl in sls]
    m_low = [_mm(bs_t[q], ta[q]) for q in qs_]
    g_new = [_mm(bs_t[q], tav[q]) + _mm(ks_t[q], sv[q]) for q in qs_]
    qmat = [rt[:, sls[q]] + unstack(_mm(a_rb[q], ta[q])) for q in qs_]
    y_in = [unstack(_mm(a_rb[q], tav[q]) + _mm(a_rk[q], sv[q])) for q in qs_]
    for q in qs_:
        h0 = h_scr[q]
        gam_col = jnp.sum(jnp.where(diag, gam[:, sls[q]], 0.0), axis=1, keepdims=True)
        y_scr[:, sls[q]] = y_in[q] + _mm(qmat[q], h0)
        h_scr[q] = gam_col * h0 + _mm(m_low[q], h0) + g_new[q]

    y = y_scr[...]
    inv_n = 1.0 / RWKV_HEAD
    mu = head_sum(y) * inv_n
    yc = y - mu
    var = head_sum(yc * yc) * inv_n
    yn = yc * lax.rsqrt(var + GN_EPS) * vec_ref[4:5, :] + vec_ref[5:6, :]
    bonus = head_sum(r * kmod * rk_ref[...]) * v
    o_ref[0] = ((yn + bonus) * _silu(g_ref[0])).astype(o_ref.dtype)


def _rwkv_branch(zm, zs, mu_rkv, mu_lora, vec, w_up, a_up, r_k, cols):
    b, s, _ = zm.shape
    dim = mu_rkv.shape[1]
    c = RWKV_CHUNK
    cr, ck, cv, cg = cols
    zero = jnp.zeros((LORA_DIM, dim), F32)
    wcat = jnp.concatenate([jnp.concatenate([w_up, zero], axis=1),
                            jnp.concatenate([zero, a_up], axis=1)], axis=0)
    tri = jnp.asarray(np.tril(np.ones((c, c), np.float32)), BF16)
    hid = np.arange(QUAD) // RWKV_HEAD
    ones_bd = jnp.asarray((hid[:, None] == hid[None, :]).astype(np.float32), BF16)
    const = lambda shape: pl.BlockSpec(shape, lambda bi, i: (0,) * len(shape))
    return pl.pallas_call(
        _rwkv_kernel,
        grid=(b, s // c),
        in_specs=[pl.BlockSpec((1, c, dim), lambda bi, i: (bi, i, cr)),
                  pl.BlockSpec((1, c, dim), lambda bi, i: (bi, i, ck)),
                  pl.BlockSpec((1, c, dim), lambda bi, i: (bi, i, cv)),
                  pl.BlockSpec((1, c, dim), lambda bi, i: (bi, i, cg)),
                  pl.BlockSpec((1, c, 2 * LORA_DIM), lambda bi, i: (bi, i, 0)),
                  const((3, dim)), const((1, 2 * LORA_DIM)), const((6, dim)), const((1, dim)),
                  const((2 * LORA_DIM, 2 * dim)), const((c, c)), const((QUAD, QUAD))],
        out_specs=pl.BlockSpec((1, c, dim), lambda bi, i: (bi, i, 0)),
        out_shape=jax.ShapeDtypeStruct((b, s, dim), BF16),
        scratch_shapes=[pltpu.VMEM((3, 8, dim), F32),
                        pltpu.VMEM((8, 2 * LORA_DIM), F32),
                        pltpu.VMEM((dim // QUAD, QUAD, QUAD), F32),
                        pltpu.VMEM((c, dim), F32)],
        compiler_params=_cparams(("arbitrary", "arbitrary")),
        name="rwkv7_chunk",
    )(zm, zm, zm, zm, zs, mu_rkv, mu_lora.reshape(1, 2 * LORA_DIM), vec, r_k.reshape(1, dim),
      wcat, tri, ones_bd)


def _out_proj_kernel(*refs, n_in, final):
    y_refs = refs[:n_in]
    w_refs = refs[n_in:2 * n_in]
    x_ref, gate_ref = refs[2 * n_in], refs[2 * n_in + 1]
    o_ref = refs[-1]
    acc = jnp.dot(y_refs[0][0], w_refs[0][...], preferred_element_type=F32)
    for y_ref, w_ref in zip(y_refs[1:], w_refs[1:]):
        acc = acc + jnp.dot(y_ref[0], w_ref[...], preferred_element_type=F32)
    xn = x_ref[0] + gate_ref[0] * acc
    if final:
        fg_ref = refs[2 * n_in + 2]
        ms = jnp.mean(xn * xn, axis=-1, keepdims=True)
        xn = xn * lax.rsqrt(ms + EPS) * fg_ref[...]
    o_ref[0] = xn


def _out_proj(ys, ws, x, gate, final_g, tm):
    b, s, d = x.shape
    n_in = len(ys)
    final = final_g is not None
    in_specs = [pl.BlockSpec((1, tm, y.shape[2]), lambda bi, i: (bi, i, 0)) for y in ys]
    in_specs += [pl.BlockSpec(w.shape, lambda bi, i: (0, 0)) for w in ws]
    in_specs += [pl.BlockSpec((1, tm, d), lambda bi, i: (bi, i, 0)),
                 pl.BlockSpec((1, 1, d), lambda bi, i: (bi, 0, 0))]
    args = list(ys) + list(ws) + [x, gate.reshape(b, 1, d)]
    if final:
        in_specs.append(pl.BlockSpec((1, d), lambda bi, i: (0, 0)))
        args.append(final_g.reshape(1, d))
    return pl.pallas_call(
        functools.partial(_out_proj_kernel, n_in=n_in, final=final),
        grid=(b, s // tm),
        in_specs=in_specs,
        out_specs=pl.BlockSpec((1, tm, d), lambda bi, i: (bi, i, 0)),
        out_shape=jax.ShapeDtypeStruct((b, s, d), F32),
        compiler_params=_cparams(("arbitrary", "arbitrary")),
        name="out_proj_final" if final else "out_proj",
    )(*args)


def _rope128(x, cos, sin_signed):
    return x * cos + pltpu.roll(x, 64, axis=1) * sin_signed


def _rope64_pair(x, cos, sin_signed):
    lane = lax.broadcasted_iota(I32, x.shape, 1)
    first_half = (lane % IDX_HEAD_DIM) < (IDX_HEAD_DIM // 2)
    partner = jnp.where(first_half, pltpu.roll(x, 96, axis=1), pltpu.roll(x, 32, axis=1))
    return x * cos + partner * sin_signed


def _rope_kv_kernel(zs_ref, pos_ref, inv_ref, sgn_ref, k_ref, v_ref, ki_ref, tab_ref):
    posf = pos_ref[0].astype(F32)
    ang128 = posf * inv_ref[0:1, :]
    ang64 = posf * inv_ref[1:2, :]
    c128 = jnp.cos(ang128)
    s128 = jnp.sin(ang128) * sgn_ref[0:1, :]
    c64 = jnp.cos(ang64)
    s64 = jnp.sin(ang64) * sgn_ref[1:2, :]
    tab_ref[0, 0] = c128
    tab_ref[0, 1] = s128
    tab_ref[0, 2] = c64
    tab_ref[0, 3] = s64
    zs = zs_ref[0]
    k_ref[0, 0] = _rope128(zs[:, 0:128], c128, s128).T.astype(BF16)
    v_ref[0, :, 0:128] = zs[:, 128:256].astype(BF16)
    v_ref[0, :, 128:256] = jnp.ones((zs.shape[0], 128), BF16)
    ki = _rope64_pair(zs[:, 256:384], c64, s64)
    lane = lax.broadcasted_iota(I32, ki.shape, 1)
    ki_ref[0, 0] = jnp.where(lane < IDX_HEAD_DIM, ki, pltpu.roll(ki, 64, axis=1)).T.astype(BF16)


def _rope_kv(zs, positions):
    b, s, ns = zs.shape
    t = 2 * KEY_TILE
    tile = pl.BlockSpec((1, 1, 128, t), lambda bi, i: (bi, i, 0, 0))
    inv128 = ROPE_THETA ** (-jnp.arange(0, ATT_HEAD_DIM, 2, dtype=F32) / ATT_HEAD_DIM)
    inv64 = ROPE_THETA ** (-jnp.arange(0, IDX_HEAD_DIM, 2, dtype=F32) / IDX_HEAD_DIM)
    inv = jnp.stack([jnp.tile(inv128, 2), jnp.tile(inv64, 4)])
    sgn128 = np.where(np.arange(128) < 64, -1.0, 1.0)
    sgn64 = np.where((np.arange(128) % 64) < 32, -1.0, 1.0)
    sgn = jnp.asarray(np.stack([sgn128, sgn64]), F32)
    blk = lambda w: pl.BlockSpec((1, t, w), lambda bi, i: (bi, i, 0))
    return pl.pallas_call(
        _rope_kv_kernel,
        grid=(b, s // t),
        in_specs=[blk(ns), blk(1),
                  pl.BlockSpec((2, 128), lambda bi, i: (0, 0)),
                  pl.BlockSpec((2, 128), lambda bi, i: (0, 0))],
        out_specs=[tile, blk(256), tile,
                   pl.BlockSpec((1, 4, t, 128), lambda bi, i: (bi, 0, i, 0))],
        out_shape=[jax.ShapeDtypeStruct((b, s // t, 128, t), BF16),
                   jax.ShapeDtypeStruct((b, s, 256), BF16),
                   jax.ShapeDtypeStruct((b, s // t, 128, t), BF16),
                   jax.ShapeDtypeStruct((b, 4, s, 128), F32)],
        compiler_params=_cparams(("arbitrary", "arbitrary")),
        name="rope_kv",
    )(zs, positions.reshape(b, s, 1), inv, sgn)


def _dsa_kernel(q_ref, g_ref, qi_ref, zs_ref, tab_ref, qpos_ref, kpos_ref, k_ref, v_ref, ki_ref,
                upper_ref, o_ref, qs_scr, qis_scr, wb_scr, key_scr, hi_scr, lo_scr, acc_scr, m_scr,
                *, k_sel):
    tq = q_ref.shape[1]
    qb = pl.program_id(1)
    nk = (qb * tq + tq + KEY_TILE - 1) // KEY_TILE
    npair = (nk + 1) // 2
    att_tile = 2 * KEY_TILE

    c128, s128 = tab_ref[0, 0], tab_ref[0, 1]
    c64, s64 = tab_ref[0, 2], tab_ref[0, 3]
    q_scale = (ATT_HEAD_DIM ** -0.5) * math.log2(math.e)
    lane = lax.broadcasted_iota(I32, (tq, 128), 1)
    eye = jnp.where(lax.broadcasted_iota(I32, (tq, 128), 0) == lane, 1.0, 0.0).astype(BF16)
    for h in range(ATT_HEADS):
        qh = _rope128(q_ref[0, :, h * 128:(h + 1) * 128], c128, s128) * q_scale
        qs_scr[h * tq:(h + 1) * tq, 0:128] = qh.astype(BF16)
        qs_scr[h * tq:(h + 1) * tq, 128:256] = eye
    zs = zs_ref[0]
    w_scale = (IDX_HEADS ** -0.5) * (IDX_HEAD_DIM ** -0.5)
    w_off = 256 + IDX_HEAD_DIM
    for p in range(IDX_HEADS // 2):
        pair = _rope64_pair(qi_ref[0, :, p * 128:(p + 1) * 128], c64, s64)
        qis_scr[(2 * p) * tq:(2 * p + 1) * tq, :] = jnp.where(lane < 64, pair, 0.0).astype(BF16)
        qis_scr[(2 * p + 1) * tq:(2 * p + 2) * tq, :] = jnp.where(lane >= 64, pair, 0.0).astype(BF16)
    for h in range(IDX_HEADS):
        wcol = zs[:, w_off + h:w_off + h + 1] * w_scale
        wb_scr[h * tq:(h + 1) * tq, :] = jnp.broadcast_to(wcol, (tq, 128))

    qchunk = qpos_ref[0] >> 6

    def score_body(jp, carry):
        kt = ki_ref[0, jp]
        nchunk = IDX_HEADS // 2

        def logits(c):
            return _mm(qis_scr[2 * c * tq:(2 * c + 2) * tq, :], kt)

        def weighted(c, sc):
            out = None
            for h in range(2):
                wb = jnp.concatenate([wb_scr[(2 * c + h) * tq:(2 * c + h + 1) * tq, :]] * (att_tile // 128),
                                     axis=1)
                term = jnp.maximum(sc[h * tq:(h + 1) * tq], 0.0) * wb
                out = term if out is None else out + term
            return out

        sc = logits(0)
        tot = None
        for c in range(nchunk):
            nxt = logits(c + 1) if c + 1 < nchunk else None
            w = weighted(c, sc)
            tot = w if tot is None else tot + w
            sc = nxt
        for t in range(2):
            kchunk = kpos_ref[0, pl.ds(2 * jp + t, 1), :] >> 6
            part = jnp.where(kchunk <= qchunk, tot[:, t * KEY_TILE:(t + 1) * KEY_TILE], -jnp.inf)
            bits = pltpu.bitcast(part, I32)
            key = bits ^ ((bits >> 31) & 0x7FFFFFFF)
            key_scr[2 * jp + t] = key
            hi_scr[2 * jp + t] = (key >> 16).astype(I16)
        return carry

    lax.fori_loop(0, npair, score_body, 0)
    neg_key = INT_MIN + 0x7FFFFF

    one16 = jnp.ones((tq, KEY_TILE), I16)
    zero16 = jnp.zeros((tq, KEY_TILE), I16)

    def to16(col):
        return jnp.broadcast_to(col, (tq, KEY_TILE)).astype(I16)

    def count16(plane, cand16, strict):
        cmp = (lambda x: x > cand16) if strict else (lambda x: x >= cand16)

        def body(jp, acc):
            acc = acc + jnp.where(cmp(plane[2 * jp]), one16, zero16)
            return acc + jnp.where(cmp(plane[2 * jp + 1]), one16, zero16)

        acc = lax.fori_loop(0, npair, body, zero16)
        return jnp.sum(acc.astype(I32), axis=1, keepdims=True)

    def kth_largest16(plane, k_need):
        def bit_body(i, thr):
            cand = thr + jnp.left_shift(jnp.int32(1), 15 - i)
            cnt = count16(plane, to16(cand), False)
            return jnp.where(cnt >= k_need, cand, thr)
        return lax.fori_loop(0, 16, bit_body, jnp.full((tq, 1), MIN16, I32))

    thi = kth_largest16(hi_scr, k_sel)
    n_above = count16(hi_scr, to16(thi), True)

    def lo_body(j, carry):
        key = key_scr[j]
        lo = (key & 0xFFFF) - 32768
        lo_scr[j] = jnp.where((key >> 16) == thi, lo, MIN16).astype(I16)
        return carry

    lax.fori_loop(0, 2 * npair, lo_body, 0)
    tlo = kth_largest16(lo_scr, k_sel - n_above)
    n_gt = n_above + count16(lo_scr, to16(tlo), True)
    thr = thi * 65536 + (tlo + 32768)
    need = (k_sel - n_gt).astype(F32)

    acc_scr[...] = jnp.zeros(acc_scr.shape, F32)
    m_scr[...] = jnp.full(m_scr.shape, NEG_BIG, F32)
    upper = upper_ref[...]
    nchunk = ATT_HEADS // 2
    crow = 2 * tq

    def mask_bias(step, eqc):
        st = jnp.minimum(step, npair - 1)
        parts = []
        for t in range(2):
            key = key_scr[2 * st + t]
            eqf = jnp.where(key == thr, 1.0, 0.0)
            before = _mm(eqf.astype(BF16), upper) + eqc
            take = jnp.where(key > thr, 1.0, jnp.where(before < need, eqf, 0.0))
            parts.append(jnp.where((take > 0.0) & (key > neg_key), 0.0, NEG_BIG).astype(BF16))
            eqc = eqc + jnp.sum(eqf, axis=1, keepdims=True)
        return jnp.concatenate(parts, axis=1), eqc

    def att_body(jp, carry):
        eqc, bias = carry
        rhs = jnp.concatenate([k_ref[0, jp], bias], axis=0)
        vt = v_ref[0, pl.ds(pl.multiple_of(jp * att_tile, att_tile), att_tile), :]

        def logits(c):
            return _mm(qs_scr[c * crow:(c + 1) * crow, :], rhs)

        def softmax(c, s):
            ps, alphas = [], []
            for h in range(2):
                rs = slice((2 * c + h) * tq, (2 * c + h + 1) * tq)
                sh = s[h * tq:(h + 1) * tq]
                m_old = m_scr[rs, :]
                m_new = jnp.maximum(m_old, jnp.max(sh, axis=1, keepdims=True))
                ps.append(jnp.exp2(sh - jnp.concatenate([m_new] * (att_tile // 128), axis=1)).astype(BF16))
                alphas.append(jnp.exp2(m_old - m_new))
                m_scr[rs, :] = m_new
            return jnp.concatenate(ps, axis=0), jnp.concatenate(alphas, axis=0)

        def accumulate(c, p, alpha):
            rs = slice(c * crow, (c + 1) * crow)
            acc_scr[rs, :] = jnp.concatenate([alpha, alpha], axis=1) * acc_scr[rs, :] + _mm(p, vt)

        s = logits(0)
        prev = None
        for c in range(nchunk):
            nxt = logits(c + 1) if c + 1 < nchunk else None
            cur = softmax(c, s)
            if prev is not None:
                accumulate(c - 1, *prev)
            if c == 1:
                bias_next, eqc = mask_bias(jp + 1, eqc)
            prev, s = cur, nxt
        accumulate(nchunk - 1, *prev)
        return eqc, bias_next

    bias0, eqc0 = mask_bias(0, jnp.zeros((tq, 1), F32))
    lax.fori_loop(0, npair, att_body, (eqc0, bias0))
    o = acc_scr[:, 0:128] / acc_scr[:, 128:256]
    for h in range(ATT_HEADS):
        sl = slice(h * 128, (h + 1) * 128)
        o_ref[0, :, sl] = (o[h * tq:(h + 1) * tq, :] * _silu(g_ref[0, :, sl])).astype(o_ref.dtype)


def _dsa(zm, zs, tabs, positions, k_rot, v_b, ki2):
    b, s, _ = zm.shape
    tq = Q_BLOCK
    att = ATT_HEADS * ATT_HEAD_DIM
    idx = IDX_HEADS * IDX_HEAD_DIM
    k_sel = min(TOPK_MAX, s // 4)
    nkt = s // KEY_TILE
    assert nkt % 2 == 0, "key tiles are consumed in pairs"
    rows = ATT_HEADS * tq
    upper = jnp.asarray(np.triu(np.ones((KEY_TILE, KEY_TILE), np.float32), 1), BF16)
    tiles = pl.BlockSpec((1, nkt // 2, 128, 2 * KEY_TILE), lambda bi, i: (bi, 0, 0, 0))
    return pl.pallas_call(
        functools.partial(_dsa_kernel, k_sel=k_sel),
        grid=(b, s // tq),
        in_specs=[pl.BlockSpec((1, tq, att), lambda bi, i: (bi, i, 0)),
                  pl.BlockSpec((1, tq, att), lambda bi, i: (bi, i, 1)),
                  pl.BlockSpec((1, tq, idx), lambda bi, i: (bi, i, 2 * att // idx)),
                  pl.BlockSpec((1, tq, zs.shape[2]), lambda bi, i: (bi, i, 0)),
                  pl.BlockSpec((1, 4, tq, 128), lambda bi, i: (bi, 0, i, 0)),
                  pl.BlockSpec((1, tq, 1), lambda bi, i: (bi, i, 0)),
                  pl.BlockSpec((1, nkt, KEY_TILE), lambda bi, i: (bi, 0, 0)),
                  tiles, pl.BlockSpec((1, s, 256), lambda bi, i: (bi, 0, 0)), tiles,
                  pl.BlockSpec((KEY_TILE, KEY_TILE), lambda bi, i: (0, 0))],
        out_specs=pl.BlockSpec((1, tq, att), lambda bi, i: (bi, i, 0)),
        out_shape=jax.ShapeDtypeStruct((b, s, att), BF16),
        scratch_shapes=[pltpu.VMEM((rows, 256), BF16),
                        pltpu.VMEM((IDX_HEADS * tq, 128), BF16),
                        pltpu.VMEM((IDX_HEADS * tq, 128), F32),
                        pltpu.VMEM((nkt, tq, KEY_TILE), I32),
                        pltpu.VMEM((nkt, tq, KEY_TILE), I16),
                        pltpu.VMEM((nkt, tq, KEY_TILE), I16),
                        pltpu.VMEM((rows, 256), F32),
                        pltpu.VMEM((rows, 128), F32)],
        compiler_params=_cparams(("arbitrary", "arbitrary")),
        name="dsa_attention",
    )(zm, zm, zm, zs, tabs, positions.reshape(b, s, 1), positions.reshape(b, nkt, KEY_TILE),
      k_rot, v_b, ki2, upper)


def _even_layer(x, scale, shift, gate, g, w_in, w_out, conv_w, conv_vec, mu_rkv, mu_lora, vec,
                w_up, a_up, r_k, final_g):
    d = x.shape[2]
    n6 = 6 * d
    lora = 2 * LORA_DIM
    w_main = jnp.concatenate([w_in[:, :n6], w_in[:, n6 + lora:]], axis=1).astype(BF16)
    w_small = w_in[:, n6:n6 + lora].astype(BF16)
    zm, zs = _norm_proj(x, g, scale, shift, w_main, w_small, tm=min(PROJ_ROWS, x.shape[1]), tn=1024)
    ya = _conv_branch(zm, conv_w[:, 0, :], conv_vec, cols=(0, 1, 2), t=min(CONV_ROWS, x.shape[1]))
    yb = _rwkv_branch(zm, zs, mu_rkv, mu_lora, vec, w_up, a_up, r_k, cols=(3, 4, 5, 6))
    wo = w_out.astype(BF16)
    return _out_proj([ya, yb], [wo[:d], wo[d:]], x, gate, final_g, tm=min(OUT_ROWS, x.shape[1]))


def _odd_layer(x, scale, shift, gate, g, w_in, w_out, positions, final_g):
    att = ATT_HEADS * ATT_HEAD_DIM
    idx = IDX_HEADS * IDX_HEAD_DIM
    o_q, o_k, o_v = 0, att, att + 128
    o_qi = o_v + 128
    o_ki = o_qi + idx
    o_wi = o_ki + IDX_HEAD_DIM
    o_g = o_wi + IDX_HEADS
    w_main = jnp.concatenate([w_in[:, o_q:o_k], w_in[:, o_g:], w_in[:, o_qi:o_ki]], axis=1).astype(BF16)
    pad = jnp.zeros((w_in.shape[0], 128 - IDX_HEAD_DIM - IDX_HEADS), w_in.dtype)
    w_small = jnp.concatenate([w_in[:, o_k:o_qi], w_in[:, o_ki:o_g], pad], axis=1).astype(BF16)
    zm, zs = _norm_proj(x, g, scale, shift, w_main, w_small, tm=min(PROJ_ROWS, x.shape[1]), tn=1536)
    k_rot, v_b, ki2, tabs = _rope_kv(zs, positions)
    o = _dsa(zm, zs, tabs, positions, k_rot, v_b, ki2)
    return _out_proj([o], [w_out.astype(BF16)], x, gate, final_g, tm=min(OUT_ROWS, x.shape[1]))


def kernel(x, c, positions, ada_w, ada_b, norm_g, final_g, even_w_in, even_w_out, conv_w, conv_vec,
           rwkv_mu_rkv, rwkv_mu_lora, rwkv_vec, rwkv_w_up, rwkv_a_up, rwkv_r_k, odd_w_in, odd_w_out):
    depth = ada_w.shape[0]
    d = x.shape[2]
    mod = _modulation(c, ada_w, ada_b)
    for l in range(depth):
        shift, scale, gate = mod[l, :, :d], mod[l, :, d:2 * d], mod[l, :, 2 * d:]
        fg = final_g if l == depth - 1 else None
        j = l // 2
        if l % 2 == 0:
            x = _even_layer(x, scale, shift, gate, norm_g[l], even_w_in[j], even_w_out[j], conv_w[j],
                            conv_vec[j], rwkv_mu_rkv[j], rwkv_mu_lora[j], rwkv_vec[j], rwkv_w_up[j],
                            rwkv_a_up[j], rwkv_r_k[j], fg)
        else:
            x = _odd_layer(x, scale, shift, gate, norm_g[l], odd_w_in[j], odd_w_out[j], positions, fg)
    return x
```

```python
import functools
import math

import jax
import jax.numpy as jnp
import numpy as np
from jax import lax
from jax.experimental import pallas as pl
from jax.experimental.pallas import tpu as pltpu

F32 = jnp.float32
BF16 = jnp.bfloat16
I32 = jnp.int32
I16 = jnp.int16

CHUNK = 64
EPS = 1e-6
LN_EPS = 1e-5
ROPE_THETA = 10000.0
CONV_WIDTH = 31
CONV_HALO = 32
SUBLANES = 8
LANES = 128
CONV_ACC_ROWS = 128
ATT_CHUNK_HEADS = 4
RWKV_BATCH_ROWS = 2
RWKV_HEAD = 64
LORA_DIM = 64
GN_EPS = 64e-5
ATT_HEADS = 16
ATT_HEAD_DIM = 128
IDX_HEADS = 8
IDX_HEAD_DIM = 64
TOPK_MAX = 256
Q_BLOCK = 128
KEY_TILE = 256
QUAD = 256
HEADS_PER_QUAD = QUAD // RWKV_HEAD
RWKV_CHUNK = QUAD // HEADS_PER_QUAD
NEG_BIG = -1e30
INT_MIN = -(2 ** 31)
MIN16 = -(2 ** 15)
VMEM_LIMIT = 56 * 1024 * 1024
PROJ_ROWS = 1024
OUT_ROWS = 512
CONV_ROWS = 256


def _cparams(sem):
    return pltpu.CompilerParams(dimension_semantics=sem, vmem_limit_bytes=VMEM_LIMIT)


def _split_bf16(x, n):
    if x.dtype == BF16:
        return [x]
    parts = []
    rem = x
    for i in range(n):
        p = rem.astype(BF16)
        parts.append(p)
        if i + 1 < n:
            rem = rem - p.astype(F32)
    return parts


def _mm(a, b, pa=1, pb=1, trans_b=False):
    ap = _split_bf16(a, pa)
    bp = _split_bf16(b, pb)
    dn = (((1,), (1 if trans_b else 0,)), ((), ()))
    order = max(len(ap), len(bp))
    acc = None
    for i, x in enumerate(ap):
        for j, y in enumerate(bp):
            if i + j >= order:
                continue
            t = lax.dot_general(x, y, dn, preferred_element_type=F32)
            acc = t if acc is None else acc + t
    return acc


def _sigmoid(x):
    return 1.0 / (1.0 + jnp.exp(-x))


def _silu(x):
    return x * _sigmoid(x)


def _mod_kernel(c_ref, w_ref, b_ref, o_ref):
    cond = _silu(c_ref[...])
    o_ref[0] = _mm(cond, w_ref[0], 2, 2) + b_ref[0]


def _modulation(c, ada_w, ada_b):
    depth, d, n = ada_w.shape
    b = c.shape[0]
    tn = 1024
    return pl.pallas_call(
        _mod_kernel,
        grid=(depth, n // tn),
        in_specs=[pl.BlockSpec((b, d), lambda l, j: (0, 0)),
                  pl.BlockSpec((1, d, tn), lambda l, j: (l, 0, j)),
                  pl.BlockSpec((1, 1, tn), lambda l, j: (l, 0, j))],
        out_specs=pl.BlockSpec((1, b, tn), lambda l, j: (l, 0, j)),
        out_shape=jax.ShapeDtypeStruct((depth, b, n), F32),
        compiler_params=_cparams(("arbitrary", "arbitrary")),
        name="adaln_mod",
    )(c, ada_w, ada_b.reshape(depth, 1, n))


def _norm_proj_kernel(x_ref, g_ref, sc_ref, sh_ref, wm_ref, ws_ref, zm_ref, zs_ref, h_scr):
    @pl.when(pl.program_id(2) == 0)
    def _():
        x = x_ref[0]
        ms = jnp.mean(x * x, axis=-1, keepdims=True)
        y = x * lax.rsqrt(ms + EPS) * g_ref[...]
        h = (y * (1.0 + sc_ref[0]) + sh_ref[0]).astype(BF16)
        h_scr[...] = h
        zs_ref[0] = jnp.dot(h, ws_ref[...], preferred_element_type=F32)

    zm_ref[0] = jnp.dot(h_scr[...], wm_ref[...], preferred_element_type=F32)


def _norm_proj(x, g, scale, shift, w_main, w_small, tm, tn):
    b, s, d = x.shape
    nm = w_main.shape[1]
    ns = w_small.shape[1]
    return pl.pallas_call(
        _norm_proj_kernel,
        grid=(b, s // tm, nm // tn),
        in_specs=[pl.BlockSpec((1, tm, d), lambda bi, i, j: (bi, i, 0)),
                  pl.BlockSpec((1, d), lambda bi, i, j: (0, 0)),
                  pl.BlockSpec((1, 1, d), lambda bi, i, j: (bi, 0, 0)),
                  pl.BlockSpec((1, 1, d), lambda bi, i, j: (bi, 0, 0)),
                  pl.BlockSpec((d, tn), lambda bi, i, j: (0, j)),
                  pl.BlockSpec((d, ns), lambda bi, i, j: (0, 0))],
        out_specs=[pl.BlockSpec((1, tm, tn), lambda bi, i, j: (bi, i, j)),
                   pl.BlockSpec((1, tm, ns), lambda bi, i, j: (bi, i, 0))],
        out_shape=[jax.ShapeDtypeStruct((b, s, nm), F32),
                   jax.ShapeDtypeStruct((b, s, ns), F32)],
        scratch_shapes=[pltpu.VMEM((tm, d), BF16)],
        compiler_params=_cparams(("arbitrary", "arbitrary", "arbitrary")),
        name="norm_proj",
    )(x, g.reshape(1, d), scale.reshape(b, 1, d), shift.reshape(b, 1, d), w_main, w_small)


def _conv_kernel(val_ref, glu_ref, gate_ref, w_ref, vec_ref, o_ref, u_scr, sh_scr, y_scr):
    t = val_ref.shape[1]
    c = val_ref.shape[2]

    @pl.when(pl.program_id(1) == 0)
    def _():
        u_scr[0:CONV_HALO, :] = jnp.zeros((CONV_HALO, u_scr.shape[1]), F32)

    @pl.when(pl.program_id(1) > 0)
    def _():
        u_scr[0:CONV_HALO, :] = u_scr[t:t + CONV_HALO, :]

    u_scr[CONV_HALO:CONV_HALO + t, :] = val_ref[0] * _sigmoid(glu_ref[0])
    ext = sh_scr.shape[1]
    for a in range(1, SUBLANES):
        sh_scr[a - 1] = u_scr[pl.ds(a, ext), :]
    first = CONV_HALO - (CONV_WIDTH - 1)
    rb = CONV_ACC_ROWS
    for cs in range(c // LANES):
        ls = slice(cs * LANES, (cs + 1) * LANES)
        for r0 in range(0, t, rb):
            acc = None
            for j in range(CONV_WIDTH):
                a, k8 = (first + j) % SUBLANES, (first + j) // SUBLANES * SUBLANES
                rows = slice(k8 + r0, k8 + r0 + rb)
                blk = u_scr[rows, ls] if a == 0 else sh_scr[a - 1, rows, ls]
                term = w_ref[j:j + 1, ls] * blk
                acc = term if acc is None else acc + term
            y_scr[r0:r0 + rb, ls] = acc + vec_ref[0:1, ls]
    y = y_scr[...]
    mu = jnp.mean(y, axis=-1, keepdims=True)
    yc = y - mu
    var = jnp.mean(yc * yc, axis=-1, keepdims=True)
    yn = yc * lax.rsqrt(var + LN_EPS) * vec_ref[1:2, :] + vec_ref[2:3, :]
    o_ref[0] = (_silu(yn) * _silu(gate_ref[0])).astype(o_ref.dtype)


def _conv_branch(zm, conv_w, conv_vec, cols, t):
    b, s, _ = zm.shape
    c = conv_w.shape[1]
    cv, cg, cs = cols
    return pl.pallas_call(
        _conv_kernel,
        grid=(b, s // t),
        in_specs=[pl.BlockSpec((1, t, c), lambda bi, i: (bi, i, cv)),
                  pl.BlockSpec((1, t, c), lambda bi, i: (bi, i, cg)),
                  pl.BlockSpec((1, t, c), lambda bi, i: (bi, i, cs)),
                  pl.BlockSpec((CONV_WIDTH, c), lambda bi, i: (0, 0)),
                  pl.BlockSpec((3, c), lambda bi, i: (0, 0))],
        out_specs=pl.BlockSpec((1, t, c), lambda bi, i: (bi, i, 0)),
        out_shape=jax.ShapeDtypeStruct((b, s, c), BF16),
        scratch_shapes=[pltpu.VMEM((CONV_HALO + t, c), F32),
                        pltpu.VMEM((SUBLANES - 1, t + CONV_HALO - SUBLANES, c), F32),
                        pltpu.VMEM((t, c), F32)],
        compiler_params=_cparams(("arbitrary", "arbitrary")),
        name="conformer_conv",
    )(zm, zm, zm, conv_w, conv_vec)


def _token_shift(x, carry_row):
    rolled = pltpu.roll(x, 1, axis=0)
    row = lax.broadcasted_iota(I32, x.shape, 0)
    return jnp.where(row == 0, carry_row, rolled)


def _rwkv_kernel(r_ref, k_ref, v_ref, g_ref, lo_ref, mu_ref, mul_ref, vec_ref, rk_ref, wcat_ref,
                 tri_ref, ones_ref, o_ref, carry_scr, carryl_scr, h_scr, y_scr):
    nb = r_ref.shape[0]
    c = r_ref.shape[1]
    dim = r_ref.shape[2]
    nquad = dim // QUAD

    @pl.when(pl.program_id(1) == 0)
    def _():
        carry_scr[...] = jnp.zeros(carry_scr.shape, F32)
        carryl_scr[...] = jnp.zeros(carryl_scr.shape, F32)
        h_scr[...] = jnp.zeros(h_scr.shape, F32)

    ones_bd = ones_ref[...]

    def head_sum(x):
        return jnp.concatenate(
            [_mm(x[:, q * QUAD:(q + 1) * QUAD], ones_bd, 2, 1) for q in range(nquad)], axis=1)

    def prepare(bb):
        def lerp(idx, ref, mu_row):
            x = ref[bb]
            prev = _token_shift(x, carry_scr[bb, idx, 0:1, :])
            carry_scr[bb, idx, 0:1, :] = x[c - 1:c, :]
            return x + (prev - x) * mu_row

        r = lerp(0, r_ref, mu_ref[0:1, :])
        k = lerp(1, k_ref, mu_ref[1:2, :])
        v = lerp(2, v_ref, mu_ref[2:3, :])
        lo0 = lo_ref[bb]
        lo_prev = _token_shift(lo0, carryl_scr[bb, 0:1, :])
        carryl_scr[bb, 0:1, :] = lo0[c - 1:c, :]
        lo = lo0 + (lo_prev - lo0) * mul_ref[...]
        lane = lax.broadcasted_iota(I32, lo.shape, 1)
        lo_act = jnp.where(lane < LORA_DIM, jnp.tanh(lo), lo)
        pre = _mm(lo_act, wcat_ref[...], 2, 2)
        w_pre = pre[:, :dim] + vec_ref[0:1, :]
        a_pre = pre[:, dim:] + vec_ref[1:2, :]
        sp = jnp.maximum(-w_pre, 0.0) + jnp.log(1.0 + jnp.exp(-jnp.abs(w_pre)))
        logw = -jnp.exp(-sp - 0.5)
        a = _sigmoid(a_pre)
        kk = k * vec_ref[2:3, :]
        kk = kk / jnp.maximum(jnp.sqrt(head_sum(kk * kk)), 1e-12)
        kmod = k * (1.0 + (a - 1.0) * vec_ref[3:4, :])
        kka = kk * a
        cum = _mm(tri_ref[...], logw, 1, 3)
        tot = cum[c - 1:c, :]
        e_neg = jnp.exp(-cum)
        e_rem = jnp.exp(tot - cum)
        return dict(r=r, v=v, kmod=kmod, at=-kk * jnp.exp(cum - logw), bt=kka * e_neg, kt=kmod * e_neg,
                    rt=r * jnp.exp(cum), bh=kka * e_rem, kh=kmod * e_rem, gam=jnp.exp(tot))

    prep = [prepare(bb) for bb in range(nb)]

    lane_h = lax.broadcasted_iota(I32, (c, QUAD), 1) // RWKV_HEAD
    n4 = HEADS_PER_QUAD * c
    row = lax.broadcasted_iota(I32, (n4, n4), 0)
    col = lax.broadcasted_iota(I32, (n4, n4), 1)
    strict = row > col
    incl = row >= col
    diag = row == col

    def stack(z):
        return jnp.concatenate([jnp.where(lane_h == h, z, 0.0) for h in range(HEADS_PER_QUAD)], axis=0)

    def unstack(zs):
        out = zs[0:c]
        for h in range(1, HEADS_PER_QUAD):
            out = out + zs[h * c:(h + 1) * c]
        return out

    chains = [(bb, q) for bb in range(nb) for q in range(nquad)]
    ids = range(len(chains))

    def part(name, i):
        bb, q = chains[i]
        return prep[bb][name][:, q * QUAD:(q + 1) * QUAD]

    sa = [stack(part("at", i)) for i in ids]
    sv = [stack(part("v", i)) for i in ids]
    aa = [_mm(jnp.concatenate([sa[i], stack(part("rt", i))], axis=0),
              jnp.concatenate([stack(part("bt", i)), stack(part("kt", i))], axis=0), trans_b=True)
          for i in ids]
    n_ab = [jnp.where(strict, x[0:n4, 0:n4], 0.0) for x in aa]
    a_ak = [jnp.where(strict, x[0:n4, n4:], 0.0) for x in aa]
    a_rb = [jnp.where(incl, x[n4:, 0:n4], 0.0) for x in aa]
    a_rk = [jnp.where(incl, x[n4:, n4:], 0.0) for x in aa]
    tm = [jnp.where(diag, 1.0, x) for x in n_ab]
    p = n_ab
    for _ in range(int(math.log2(c)) - 1):
        p = [_mm(x, x) for x in p]
        tm = [t + _mm(t, x) for t, x in zip(tm, p)]
    ta = [_mm(tm[i], sa[i]) for i in ids]
    av = [_mm(a_ak[i], sv[i]) for i in ids]
    tav = [_mm(tm[i], av[i]) for i in ids]
    bs_t = [stack(part("bh", i)).T for i in ids]
    ks_t = [stack(part("kh", i)).T for i in ids]
    m_low = [_mm(bs_t[i], ta[i]) for i in ids]
    g_new = [_mm(bs_t[i], tav[i]) + _mm(ks_t[i], sv[i]) for i in ids]
    qmat = [part("rt", i) + unstack(_mm(a_rb[i], ta[i])) for i in ids]
    y_in = [unstack(_mm(a_rb[i], tav[i]) + _mm(a_rk[i], sv[i])) for i in ids]
    for i in ids:
        bb, q = chains[i]
        h0 = h_scr[bb, q]
        gam_col = jnp.sum(jnp.where(diag, part("gam", i), 0.0), axis=1, keepdims=True)
        y_scr[bb, :, q * QUAD:(q + 1) * QUAD] = y_in[i] + _mm(qmat[i], h0)
        h_scr[bb, q] = gam_col * h0 + _mm(m_low[i], h0) + g_new[i]

    inv_n = 1.0 / RWKV_HEAD
    for bb in range(nb):
        y = y_scr[bb]
        mu = head_sum(y) * inv_n
        yc = y - mu
        var = head_sum(yc * yc) * inv_n
        yn = yc * lax.rsqrt(var + GN_EPS) * vec_ref[4:5, :] + vec_ref[5:6, :]
        bonus = head_sum(prep[bb]["r"] * prep[bb]["kmod"] * rk_ref[...]) * prep[bb]["v"]
        o_ref[bb] = ((yn + bonus) * _silu(g_ref[bb])).astype(o_ref.dtype)


def _rwkv_branch(zm, zs, mu_rkv, mu_lora, vec, w_up, a_up, r_k, cols):
    b, s, _ = zm.shape
    dim = mu_rkv.shape[1]
    c = RWKV_CHUNK
    cr, ck, cv, cg = cols
    zero = jnp.zeros((LORA_DIM, dim), F32)
    wcat = jnp.concatenate([jnp.concatenate([w_up, zero], axis=1),
                            jnp.concatenate([zero, a_up], axis=1)], axis=0)
    tri = jnp.asarray(np.tril(np.ones((c, c), np.float32)), BF16)
    hid = np.arange(QUAD) // RWKV_HEAD
    ones_bd = jnp.asarray((hid[:, None] == hid[None, :]).astype(np.float32), BF16)
    const = lambda shape: pl.BlockSpec(shape, lambda bi, i: (0,) * len(shape))
    nb = RWKV_BATCH_ROWS if b % RWKV_BATCH_ROWS == 0 else 1
    return pl.pallas_call(
        _rwkv_kernel,
        grid=(b // nb, s // c),
        in_specs=[pl.BlockSpec((nb, c, dim), lambda bi, i: (bi, i, cr)),
                  pl.BlockSpec((nb, c, dim), lambda bi, i: (bi, i, ck)),
                  pl.BlockSpec((nb, c, dim), lambda bi, i: (bi, i, cv)),
                  pl.BlockSpec((nb, c, dim), lambda bi, i: (bi, i, cg)),
                  pl.BlockSpec((nb, c, 2 * LORA_DIM), lambda bi, i: (bi, i, 0)),
                  const((3, dim)), const((1, 2 * LORA_DIM)), const((6, dim)), const((1, dim)),
                  const((2 * LORA_DIM, 2 * dim)), const((c, c)), const((QUAD, QUAD))],
        out_specs=pl.BlockSpec((nb, c, dim), lambda bi, i: (bi, i, 0)),
        out_shape=jax.ShapeDtypeStruct((b, s, dim), BF16),
        scratch_shapes=[pltpu.VMEM((nb, 3, 8, dim), F32),
                        pltpu.VMEM((nb, 8, 2 * LORA_DIM), F32),
                        pltpu.VMEM((nb, dim // QUAD, QUAD, QUAD), F32),
                        pltpu.VMEM((nb, c, dim), F32)],
        compiler_params=_cparams(("arbitrary", "arbitrary")),
        name="rwkv7_chunk",
    )(zm, zm, zm, zm, zs, mu_rkv, mu_lora.reshape(1, 2 * LORA_DIM), vec, r_k.reshape(1, dim),
      wcat, tri, ones_bd)


def _out_proj_kernel(*refs, n_in, final):
    y_refs = refs[:n_in]
    w_refs = refs[n_in:2 * n_in]
    x_ref, gate_ref = refs[2 * n_in], refs[2 * n_in + 1]
    o_ref = refs[-1]
    acc = jnp.dot(y_refs[0][0], w_refs[0][...], preferred_element_type=F32)
    for y_ref, w_ref in zip(y_refs[1:], w_refs[1:]):
        acc = acc + jnp.dot(y_ref[0], w_ref[...], preferred_element_type=F32)
    xn = x_ref[0] + gate_ref[0] * acc
    if final:
        fg_ref = refs[2 * n_in + 2]
        ms = jnp.mean(xn * xn, axis=-1, keepdims=True)
        xn = xn * lax.rsqrt(ms + EPS) * fg_ref[...]
    o_ref[0] = xn


def _out_proj(ys, ws, x, gate, final_g, tm):
    b, s, d = x.shape
    n_in = len(ys)
    final = final_g is not None
    in_specs = [pl.BlockSpec((1, tm, y.shape[2]), lambda bi, i: (bi, i, 0)) for y in ys]
    in_specs += [pl.BlockSpec(w.shape, lambda bi, i: (0, 0)) for w in ws]
    in_specs += [pl.BlockSpec((1, tm, d), lambda bi, i: (bi, i, 0)),
                 pl.BlockSpec((1, 1, d), lambda bi, i: (bi, 0, 0))]
    args = list(ys) + list(ws) + [x, gate.reshape(b, 1, d)]
    if final:
        in_specs.append(pl.BlockSpec((1, d), lambda bi, i: (0, 0)))
        args.append(final_g.reshape(1, d))
    return pl.pallas_call(
        functools.partial(_out_proj_kernel, n_in=n_in, final=final),
        grid=(b, s // tm),
        in_specs=in_specs,
        out_specs=pl.BlockSpec((1, tm, d), lambda bi, i: (bi, i, 0)),
        out_shape=jax.ShapeDtypeStruct((b, s, d), F32),
        compiler_params=_cparams(("arbitrary", "arbitrary")),
        name="out_proj_final" if final else "out_proj",
    )(*args)


def _rope128(x, cos, sin_signed):
    return x * cos + pltpu.roll(x, 64, axis=1) * sin_signed


def _rope64_pair(x, cos, sin_signed):
    lane = lax.broadcasted_iota(I32, x.shape, 1)
    first_half = (lane % IDX_HEAD_DIM) < (IDX_HEAD_DIM // 2)
    partner = jnp.where(first_half, pltpu.roll(x, 96, axis=1), pltpu.roll(x, 32, axis=1))
    return x * cos + partner * sin_signed


def _rope_kv_kernel(zs_ref, pos_ref, inv_ref, sgn_ref, k_ref, v_ref, ki_ref, tab_ref):
    posf = pos_ref[0].astype(F32)
    ang128 = posf * inv_ref[0:1, :]
    ang64 = posf * inv_ref[1:2, :]
    c128 = jnp.cos(ang128)
    s128 = jnp.sin(ang128) * sgn_ref[0:1, :]
    c64 = jnp.cos(ang64)
    s64 = jnp.sin(ang64) * sgn_ref[1:2, :]
    tab_ref[0, 0] = c128
    tab_ref[0, 1] = s128
    tab_ref[0, 2] = c64
    tab_ref[0, 3] = s64
    zs = zs_ref[0]
    k_ref[0, 0] = _rope128(zs[:, 0:128], c128, s128).T.astype(BF16)
    v_ref[0, :, 0:128] = zs[:, 128:256].astype(BF16)
    v_ref[0, :, 128:256] = jnp.ones((zs.shape[0], 128), BF16)
    ki = _rope64_pair(zs[:, 256:384], c64, s64)
    lane = lax.broadcasted_iota(I32, ki.shape, 1)
    ki_ref[0, 0] = jnp.where(lane < IDX_HEAD_DIM, ki, pltpu.roll(ki, 64, axis=1)).T.astype(BF16)


def _rope_kv(zs, positions):
    b, s, ns = zs.shape
    t = 2 * KEY_TILE
    tile = pl.BlockSpec((1, 1, 128, t), lambda bi, i: (bi, i, 0, 0))
    inv128 = ROPE_THETA ** (-jnp.arange(0, ATT_HEAD_DIM, 2, dtype=F32) / ATT_HEAD_DIM)
    inv64 = ROPE_THETA ** (-jnp.arange(0, IDX_HEAD_DIM, 2, dtype=F32) / IDX_HEAD_DIM)
    inv = jnp.stack([jnp.tile(inv128, 2), jnp.tile(inv64, 4)])
    sgn128 = np.where(np.arange(128) < 64, -1.0, 1.0)
    sgn64 = np.where((np.arange(128) % 64) < 32, -1.0, 1.0)
    sgn = jnp.asarray(np.stack([sgn128, sgn64]), F32)
    blk = lambda w: pl.BlockSpec((1, t, w), lambda bi, i: (bi, i, 0))
    return pl.pallas_call(
        _rope_kv_kernel,
        grid=(b, s // t),
        in_specs=[blk(ns), blk(1),
                  pl.BlockSpec((2, 128), lambda bi, i: (0, 0)),
                  pl.BlockSpec((2, 128), lambda bi, i: (0, 0))],
        out_specs=[tile, blk(256), tile,
                   pl.BlockSpec((1, 4, t, 128), lambda bi, i: (bi, 0, i, 0))],
        out_shape=[jax.ShapeDtypeStruct((b, s // t, 128, t), BF16),
                   jax.ShapeDtypeStruct((b, s, 256), BF16),
                   jax.ShapeDtypeStruct((b, s // t, 128, t), BF16),
                   jax.ShapeDtypeStruct((b, 4, s, 128), F32)],
        compiler_params=_cparams(("arbitrary", "arbitrary")),
        name="rope_kv",
    )(zs, positions.reshape(b, s, 1), inv, sgn)


def _dsa_kernel(q_ref, g_ref, qi_ref, zs_ref, tab_ref, qpos_ref, kpos_ref, k_ref, v_ref, ki_ref,
                upper_ref, o_ref, qs_scr, qis_scr, wb_scr, key_scr, hi_scr, lo_scr, acc_scr, m_scr,
                *, k_sel):
    tq = q_ref.shape[1]
    qb = pl.program_id(1)
    nk = (qb * tq + tq + KEY_TILE - 1) // KEY_TILE
    npair = (nk + 1) // 2
    att_tile = 2 * KEY_TILE

    c128, s128 = tab_ref[0, 0], tab_ref[0, 1]
    c64, s64 = tab_ref[0, 2], tab_ref[0, 3]
    q_scale = (ATT_HEAD_DIM ** -0.5) * math.log2(math.e)
    lane = lax.broadcasted_iota(I32, (tq, 128), 1)
    eye = jnp.where(lax.broadcasted_iota(I32, (tq, 128), 0) == lane, 1.0, 0.0).astype(BF16)
    for h in range(ATT_HEADS):
        qh = _rope128(q_ref[0, :, h * 128:(h + 1) * 128], c128, s128) * q_scale
        qs_scr[h * tq:(h + 1) * tq, 0:128] = qh.astype(BF16)
        qs_scr[h * tq:(h + 1) * tq, 128:256] = eye
    zs = zs_ref[0]
    w_scale = (IDX_HEADS ** -0.5) * (IDX_HEAD_DIM ** -0.5)
    w_off = 256 + IDX_HEAD_DIM
    for p in range(IDX_HEADS // 2):
        pair = _rope64_pair(qi_ref[0, :, p * 128:(p + 1) * 128], c64, s64)
        qis_scr[(2 * p) * tq:(2 * p + 1) * tq, :] = jnp.where(lane < 64, pair, 0.0).astype(BF16)
        qis_scr[(2 * p + 1) * tq:(2 * p + 2) * tq, :] = jnp.where(lane >= 64, pair, 0.0).astype(BF16)
    for h in range(IDX_HEADS):
        wcol = zs[:, w_off + h:w_off + h + 1] * w_scale
        wb_scr[h * tq:(h + 1) * tq, :] = jnp.broadcast_to(wcol, (tq, 128))

    qchunk = qpos_ref[0] >> 6

    def score_body(jp, carry):
        kt = ki_ref[0, jp]
        nchunk = IDX_HEADS // 2

        def logits(c):
            return _mm(qis_scr[2 * c * tq:(2 * c + 2) * tq, :], kt)

        def weighted(c, sc):
            out = None
            for h in range(2):
                wb = jnp.concatenate([wb_scr[(2 * c + h) * tq:(2 * c + h + 1) * tq, :]] * (att_tile // 128),
                                     axis=1)
                term = jnp.maximum(sc[h * tq:(h + 1) * tq], 0.0) * wb
                out = term if out is None else out + term
            return out

        ahead = 2
        sc = {c: logits(c) for c in range(min(ahead, nchunk))}
        tot = None
        for c in range(nchunk):
            if c + ahead < nchunk:
                sc[c + ahead] = logits(c + ahead)
            w = weighted(c, sc.pop(c))
            tot = w if tot is None else tot + w
        for t in range(2):
            kchunk = kpos_ref[0, pl.ds(2 * jp + t, 1), :] >> 6
            part = jnp.where(kchunk <= qchunk, tot[:, t * KEY_TILE:(t + 1) * KEY_TILE], -jnp.inf)
            bits = pltpu.bitcast(part, I32)
            key = bits ^ ((bits >> 31) & 0x7FFFFFFF)
            key_scr[2 * jp + t] = key
            hi_scr[2 * jp + t] = (key >> 16).astype(I16)
        return carry

    lax.fori_loop(0, npair, score_body, 0)
    neg_key = INT_MIN + 0x7FFFFF

    one16 = jnp.ones((tq, KEY_TILE), I16)
    zero16 = jnp.zeros((tq, KEY_TILE), I16)

    def to16(col):
        return jnp.broadcast_to(col, (tq, KEY_TILE)).astype(I16)

    def count16(plane, cand16, strict):
        cmp = (lambda x: x > cand16) if strict else (lambda x: x >= cand16)

        def body(jp, acc):
            acc = acc + jnp.where(cmp(plane[2 * jp]), one16, zero16)
            return acc + jnp.where(cmp(plane[2 * jp + 1]), one16, zero16)

        acc = lax.fori_loop(0, npair, body, zero16)
        return jnp.sum(acc.astype(I32), axis=1, keepdims=True)

    def kth_largest16(plane, k_need):
        def bit_body(i, thr):
            cand = thr + jnp.left_shift(jnp.int32(1), 15 - i)
            cnt = count16(plane, to16(cand), False)
            return jnp.where(cnt >= k_need, cand, thr)
        return lax.fori_loop(0, 16, bit_body, jnp.full((tq, 1), MIN16, I32))

    thi = kth_largest16(hi_scr, k_sel)
    n_above = count16(hi_scr, to16(thi), True)

    def lo_body(j, carry):
        key = key_scr[j]
        lo = (key & 0xFFFF) - 32768
        lo_scr[j] = jnp.where((key >> 16) == thi, lo, MIN16).astype(I16)
        return carry

    lax.fori_loop(0, 2 * npair, lo_body, 0)
    tlo = kth_largest16(lo_scr, k_sel - n_above)
    n_gt = n_above + count16(lo_scr, to16(tlo), True)
    thr = thi * 65536 + (tlo + 32768)
    need = (k_sel - n_gt).astype(F32)

    acc_scr[...] = jnp.zeros(acc_scr.shape, F32)
    m_scr[...] = jnp.full(m_scr.shape, NEG_BIG, F32)
    upper = upper_ref[...]
    hpc = ATT_CHUNK_HEADS
    nchunk = ATT_HEADS // hpc
    crow = hpc * tq

    def mask_bias(step, eqc):
        st = jnp.minimum(step, npair - 1)
        parts = []
        for t in range(2):
            key = key_scr[2 * st + t]
            eqf = jnp.where(key == thr, 1.0, 0.0)
            before = _mm(eqf.astype(BF16), upper) + eqc
            take = jnp.where(key > thr, 1.0, jnp.where(before < need, eqf, 0.0))
            parts.append(jnp.where((take > 0.0) & (key > neg_key), 0.0, NEG_BIG).astype(BF16))
            eqc = eqc + jnp.sum(eqf, axis=1, keepdims=True)
        return jnp.concatenate(parts, axis=1), eqc

    def att_body(jp, carry):
        eqc, bias = carry
        rhs = jnp.concatenate([k_ref[0, jp], bias], axis=0)
        vt = v_ref[0, pl.ds(pl.multiple_of(jp * att_tile, att_tile), att_tile), :]

        def logits(c):
            return _mm(qs_scr[c * crow:(c + 1) * crow, :], rhs)

        def softmax(c, s):
            ps, alphas = [], []
            for h in range(hpc):
                rs = slice((hpc * c + h) * tq, (hpc * c + h + 1) * tq)
                sh = s[h * tq:(h + 1) * tq]
                m_old = m_scr[rs, :]
                m_new = jnp.maximum(m_old, jnp.max(sh, axis=1, keepdims=True))
                ps.append(jnp.exp2(sh - jnp.concatenate([m_new] * (att_tile // 128), axis=1)).astype(BF16))
                alphas.append(jnp.exp2(m_old - m_new))
                m_scr[rs, :] = m_new
            return jnp.concatenate(ps, axis=0), jnp.concatenate(alphas, axis=0)

        def accumulate(c, p, alpha):
            rs = slice(c * crow, (c + 1) * crow)
            acc_scr[rs, :] = jnp.concatenate([alpha, alpha], axis=1) * acc_scr[rs, :] + _mm(p, vt)

        ahead = 2
        s = {c: logits(c) for c in range(min(ahead, nchunk))}
        prev = None
        for c in range(nchunk):
            if c + ahead < nchunk:
                s[c + ahead] = logits(c + ahead)
            cur = softmax(c, s.pop(c))
            if prev is not None:
                accumulate(c - 1, *prev)
            if c == 0:
                bias_next, eqc = mask_bias(jp + 1, eqc)
            prev = cur
        accumulate(nchunk - 1, *prev)
        return eqc, bias_next

    bias0, eqc0 = mask_bias(0, jnp.zeros((tq, 1), F32))
    lax.fori_loop(0, npair, att_body, (eqc0, bias0))
    o = acc_scr[:, 0:128] / acc_scr[:, 128:256]
    for h in range(ATT_HEADS):
        sl = slice(h * 128, (h + 1) * 128)
        o_ref[0, :, sl] = (o[h * tq:(h + 1) * tq, :] * _silu(g_ref[0, :, sl])).astype(o_ref.dtype)


def _dsa(zm, zs, tabs, positions, k_rot, v_b, ki2):
    b, s, _ = zm.shape
    tq = Q_BLOCK
    att = ATT_HEADS * ATT_HEAD_DIM
    idx = IDX_HEADS * IDX_HEAD_DIM
    k_sel = min(TOPK_MAX, s // 4)
    nkt = s // KEY_TILE
    assert nkt % 2 == 0, "key tiles are consumed in pairs"
    rows = ATT_HEADS * tq
    upper = jnp.asarray(np.triu(np.ones((KEY_TILE, KEY_TILE), np.float32), 1), BF16)
    tiles = pl.BlockSpec((1, nkt // 2, 128, 2 * KEY_TILE), lambda bi, i: (bi, 0, 0, 0))
    return pl.pallas_call(
        functools.partial(_dsa_kernel, k_sel=k_sel),
        grid=(b, s // tq),
        in_specs=[pl.BlockSpec((1, tq, att), lambda bi, i: (bi, i, 0)),
                  pl.BlockSpec((1, tq, att), lambda bi, i: (bi, i, 1)),
                  pl.BlockSpec((1, tq, idx), lambda bi, i: (bi, i, 2 * att // idx)),
                  pl.BlockSpec((1, tq, zs.shape[2]), lambda bi, i: (bi, i, 0)),
                  pl.BlockSpec((1, 4, tq, 128), lambda bi, i: (bi, 0, i, 0)),
                  pl.BlockSpec((1, tq, 1), lambda bi, i: (bi, i, 0)),
                  pl.BlockSpec((1, nkt, KEY_TILE), lambda bi, i: (bi, 0, 0)),
                  tiles, pl.BlockSpec((1, s, 256), lambda bi, i: (bi, 0, 0)), tiles,
                  pl.BlockSpec((KEY_TILE, KEY_TILE), lambda bi, i: (0, 0))],
        out_specs=pl.BlockSpec((1, tq, att), lambda bi, i: (bi, i, 0)),
        out_shape=jax.ShapeDtypeStruct((b, s, att), BF16),
        scratch_shapes=[pltpu.VMEM((rows, 256), BF16),
                        pltpu.VMEM((IDX_HEADS * tq, 128), BF16),
                        pltpu.VMEM((IDX_HEADS * tq, 128), F32),
                        pltpu.VMEM((nkt, tq, KEY_TILE), I32),
                        pltpu.VMEM((nkt, tq, KEY_TILE), I16),
                        pltpu.VMEM((nkt, tq, KEY_TILE), I16),
                        pltpu.VMEM((rows, 256), F32),
                        pltpu.VMEM((rows, 128), F32)],
        compiler_params=_cparams(("arbitrary", "arbitrary")),
        name="dsa_attention",
    )(zm, zm, zm, zs, tabs, positions.reshape(b, s, 1), positions.reshape(b, nkt, KEY_TILE),
      k_rot, v_b, ki2, upper)


def _even_layer(x, scale, shift, gate, g, w_in, w_out, conv_w, conv_vec, mu_rkv, mu_lora, vec,
                w_up, a_up, r_k, final_g):
    d = x.shape[2]
    n6 = 6 * d
    lora = 2 * LORA_DIM
    w_main = jnp.concatenate([w_in[:, :n6], w_in[:, n6 + lora:]], axis=1).astype(BF16)
    w_small = w_in[:, n6:n6 + lora].astype(BF16)
    zm, zs = _norm_proj(x, g, scale, shift, w_main, w_small, tm=min(PROJ_ROWS, x.shape[1]), tn=1024)
    ya = _conv_branch(zm, conv_w[:, 0, :], conv_vec, cols=(0, 1, 2), t=min(CONV_ROWS, x.shape[1]))
    yb = _rwkv_branch(zm, zs, mu_rkv, mu_lora, vec, w_up, a_up, r_k, cols=(3, 4, 5, 6))
    wo = w_out.astype(BF16)
    return _out_proj([ya, yb], [wo[:d], wo[d:]], x, gate, final_g, tm=min(OUT_ROWS, x.shape[1]))


def _odd_layer(x, scale, shift, gate, g, w_in, w_out, positions, final_g):
    att = ATT_HEADS * ATT_HEAD_DIM
    idx = IDX_HEADS * IDX_HEAD_DIM
    o_q, o_k, o_v = 0, att, att + 128
    o_qi = o_v + 128
    o_ki = o_qi + idx
    o_wi = o_ki + IDX_HEAD_DIM
    o_g = o_wi + IDX_HEADS
    w_main = jnp.concatenate([w_in[:, o_q:o_k], w_in[:, o_g:], w_in[:, o_qi:o_ki]], axis=1).astype(BF16)
    pad = jnp.zeros((w_in.shape[0], 128 - IDX_HEAD_DIM - IDX_HEADS), w_in.dtype)
    w_small = jnp.concatenate([w_in[:, o_k:o_qi], w_in[:, o_ki:o_g], pad], axis=1).astype(BF16)
    zm, zs = _norm_proj(x, g, scale, shift, w_main, w_small, tm=min(PROJ_ROWS, x.shape[1]), tn=1536)
    k_rot, v_b, ki2, tabs = _rope_kv(zs, positions)
    o = _dsa(zm, zs, tabs, positions, k_rot, v_b, ki2)
    return _out_proj([o], [w_out.astype(BF16)], x, gate, final_g, tm=min(OUT_ROWS, x.shape[1]))


def kernel(x, c, positions, ada_w, ada_b, norm_g, final_g, even_w_in, even_w_out, conv_w, conv_vec,
           rwkv_mu_rkv, rwkv_mu_lora, rwkv_vec, rwkv_w_up, rwkv_a_up, rwkv_r_k, odd_w_in, odd_w_out):
    depth = ada_w.shape[0]
    d = x.shape[2]
    mod = _modulation(c, ada_w, ada_b)
    for l in range(depth):
        shift, scale, gate = mod[l, :, :d], mod[l, :, d:2 * d], mod[l, :, 2 * d:]
        fg = final_g if l == depth - 1 else None
        j = l // 2
        if l % 2 == 0:
            x = _even_layer(x, scale, shift, gate, norm_g[l], even_w_in[j], even_w_out[j], conv_w[j],
                            conv_vec[j], rwkv_mu_rkv[j], rwkv_mu_lora[j], rwkv_vec[j], rwkv_w_up[j],
                            rwkv_a_up[j], rwkv_r_k[j], fg)
        else:
            x = _odd_layer(x, scale, shift, gate, norm_g[l], odd_w_in[j], odd_w_out[j], positions, fg)
    return x
```

```python
import functools
import math

import jax
import jax.numpy as jnp
import numpy as np
from jax import lax
from jax.experimental import pallas as pl
from jax.experimental.pallas import tpu as pltpu

F32 = jnp.float32
BF16 = jnp.bfloat16
I32 = jnp.int32
I16 = jnp.int16

CHUNK = 64
EPS = 1e-6
LN_EPS = 1e-5
ROPE_THETA = 10000.0
CONV_WIDTH = 31
CONV_HALO = 32
SUBLANES = 8
LANES = 128
CONV_ACC_ROWS = 128
ATT_CHUNK_HEADS = 8
RWKV_BATCH_ROWS = 2
RWKV_HEAD = 64
LORA_DIM = 64
GN_EPS = 64e-5
ATT_HEADS = 16
ATT_HEAD_DIM = 128
IDX_HEADS = 8
IDX_HEAD_DIM = 64
TOPK_MAX = 256
Q_BLOCK = 128
KEY_TILE = 256
QUAD = 256
HEADS_PER_QUAD = QUAD // RWKV_HEAD
RWKV_CHUNK = QUAD // HEADS_PER_QUAD
NEG_BIG = -1e30
INT_MIN = -(2 ** 31)
MIN16 = -(2 ** 15)
VMEM_LIMIT = 56 * 1024 * 1024
PROJ_ROWS = 1024
OUT_ROWS = 512
CONV_ROWS = 256


def _cparams(sem):
    return pltpu.CompilerParams(dimension_semantics=sem, vmem_limit_bytes=VMEM_LIMIT)


def _split_bf16(x, n):
    if x.dtype == BF16:
        return [x]
    parts = []
    rem = x
    for i in range(n):
        p = rem.astype(BF16)
        parts.append(p)
        if i + 1 < n:
            rem = rem - p.astype(F32)
    return parts


def _mm(a, b, pa=1, pb=1, trans_b=False):
    ap = _split_bf16(a, pa)
    bp = _split_bf16(b, pb)
    dn = (((1,), (1 if trans_b else 0,)), ((), ()))
    order = max(len(ap), len(bp))
    acc = None
    for i, x in enumerate(ap):
        for j, y in enumerate(bp):
            if i + j >= order:
                continue
            t = lax.dot_general(x, y, dn, preferred_element_type=F32)
            acc = t if acc is None else acc + t
    return acc


def _sigmoid(x):
    return 1.0 / (1.0 + jnp.exp(-x))


def _silu(x):
    return x * _sigmoid(x)


def _mod_kernel(c_ref, w_ref, b_ref, o_ref):
    cond = _silu(c_ref[...])
    o_ref[0] = _mm(cond, w_ref[0], 2, 2) + b_ref[0]


def _modulation(c, ada_w, ada_b):
    depth, d, n = ada_w.shape
    b = c.shape[0]
    tn = 1024
    return pl.pallas_call(
        _mod_kernel,
        grid=(depth, n // tn),
        in_specs=[pl.BlockSpec((b, d), lambda l, j: (0, 0)),
                  pl.BlockSpec((1, d, tn), lambda l, j: (l, 0, j)),
                  pl.BlockSpec((1, 1, tn), lambda l, j: (l, 0, j))],
        out_specs=pl.BlockSpec((1, b, tn), lambda l, j: (l, 0, j)),
        out_shape=jax.ShapeDtypeStruct((depth, b, n), F32),
        compiler_params=_cparams(("arbitrary", "arbitrary")),
        name="adaln_mod",
    )(c, ada_w, ada_b.reshape(depth, 1, n))


def _norm_proj_kernel(x_ref, g_ref, sc_ref, sh_ref, wm_ref, ws_ref, zm_ref, zs_ref, h_scr):
    @pl.when(pl.program_id(2) == 0)
    def _():
        x = x_ref[0]
        ms = jnp.mean(x * x, axis=-1, keepdims=True)
        y = x * lax.rsqrt(ms + EPS) * g_ref[...]
        h = (y * (1.0 + sc_ref[0]) + sh_ref[0]).astype(BF16)
        h_scr[...] = h
        zs_ref[0] = jnp.dot(h, ws_ref[...], preferred_element_type=F32)

    zm_ref[0] = jnp.dot(h_scr[...], wm_ref[...], preferred_element_type=F32)


def _norm_proj(x, g, scale, shift, w_main, w_small, tm, tn):
    b, s, d = x.shape
    nm = w_main.shape[1]
    ns = w_small.shape[1]
    return pl.pallas_call(
        _norm_proj_kernel,
        grid=(b, s // tm, nm // tn),
        in_specs=[pl.BlockSpec((1, tm, d), lambda bi, i, j: (bi, i, 0)),
                  pl.BlockSpec((1, d), lambda bi, i, j: (0, 0)),
                  pl.BlockSpec((1, 1, d), lambda bi, i, j: (bi, 0, 0)),
                  pl.BlockSpec((1, 1, d), lambda bi, i, j: (bi, 0, 0)),
                  pl.BlockSpec((d, tn), lambda bi, i, j: (0, j)),
                  pl.BlockSpec((d, ns), lambda bi, i, j: (0, 0))],
        out_specs=[pl.BlockSpec((1, tm, tn), lambda bi, i, j: (bi, i, j)),
                   pl.BlockSpec((1, tm, ns), lambda bi, i, j: (bi, i, 0))],
        out_shape=[jax.ShapeDtypeStruct((b, s, nm), F32),
                   jax.ShapeDtypeStruct((b, s, ns), F32)],
        scratch_shapes=[pltpu.VMEM((tm, d), BF16)],
        compiler_params=_cparams(("arbitrary", "arbitrary", "arbitrary")),
        name="norm_proj",
    )(x, g.reshape(1, d), scale.reshape(b, 1, d), shift.reshape(b, 1, d), w_main, w_small)


def _conv_kernel(val_ref, glu_ref, gate_ref, w_ref, vec_ref, o_ref, u_scr, sh_scr, y_scr):
    t = val_ref.shape[1]
    c = val_ref.shape[2]

    @pl.when(pl.program_id(1) == 0)
    def _():
        u_scr[0:CONV_HALO, :] = jnp.zeros((CONV_HALO, u_scr.shape[1]), F32)

    @pl.when(pl.program_id(1) > 0)
    def _():
        u_scr[0:CONV_HALO, :] = u_scr[t:t + CONV_HALO, :]

    u_scr[CONV_HALO:CONV_HALO + t, :] = val_ref[0] * _sigmoid(glu_ref[0])
    ext = sh_scr.shape[1]
    for a in range(1, SUBLANES):
        sh_scr[a - 1] = u_scr[pl.ds(a, ext), :]
    first = CONV_HALO - (CONV_WIDTH - 1)
    rb = CONV_ACC_ROWS
    for cs in range(c // LANES):
        ls = slice(cs * LANES, (cs + 1) * LANES)
        for r0 in range(0, t, rb):
            acc = None
            for j in range(CONV_WIDTH):
                a, k8 = (first + j) % SUBLANES, (first + j) // SUBLANES * SUBLANES
                rows = slice(k8 + r0, k8 + r0 + rb)
                blk = u_scr[rows, ls] if a == 0 else sh_scr[a - 1, rows, ls]
                term = w_ref[j:j + 1, ls] * blk
                acc = term if acc is None else acc + term
            y_scr[r0:r0 + rb, ls] = acc + vec_ref[0:1, ls]
    y = y_scr[...]
    mu = jnp.mean(y, axis=-1, keepdims=True)
    yc = y - mu
    var = jnp.mean(yc * yc, axis=-1, keepdims=True)
    yn = yc * lax.rsqrt(var + LN_EPS) * vec_ref[1:2, :] + vec_ref[2:3, :]
    o_ref[0] = (_silu(yn) * _silu(gate_ref[0])).astype(o_ref.dtype)


def _conv_branch(zm, conv_w, conv_vec, cols, t):
    b, s, _ = zm.shape
    c = conv_w.shape[1]
    cv, cg, cs = cols
    return pl.pallas_call(
        _conv_kernel,
        grid=(b, s // t),
        in_specs=[pl.BlockSpec((1, t, c), lambda bi, i: (bi, i, cv)),
                  pl.BlockSpec((1, t, c), lambda bi, i: (bi, i, cg)),
                  pl.BlockSpec((1, t, c), lambda bi, i: (bi, i, cs)),
                  pl.BlockSpec((CONV_WIDTH, c), lambda bi, i: (0, 0)),
                  pl.BlockSpec((3, c), lambda bi, i: (0, 0))],
        out_specs=pl.BlockSpec((1, t, c), lambda bi, i: (bi, i, 0)),
        out_shape=jax.ShapeDtypeStruct((b, s, c), BF16),
        scratch_shapes=[pltpu.VMEM((CONV_HALO + t, c), F32),
                        pltpu.VMEM((SUBLANES - 1, t + CONV_HALO - SUBLANES, c), F32),
                        pltpu.VMEM((t, c), F32)],
        compiler_params=_cparams(("arbitrary", "arbitrary")),
        name="conformer_conv",
    )(zm, zm, zm, conv_w, conv_vec)


def _token_shift(x, carry_row):
    rolled = pltpu.roll(x, 1, axis=0)
    row = lax.broadcasted_iota(I32, x.shape, 0)
    return jnp.where(row == 0, carry_row, rolled)


def _rwkv_kernel(r_ref, k_ref, v_ref, g_ref, lo_ref, mu_ref, mul_ref, vec_ref, rk_ref, wcat_ref,
                 tri_ref, ones_ref, o_ref, carry_scr, carryl_scr, h_scr, y_scr):
    nb = r_ref.shape[0]
    c = r_ref.shape[1]
    dim = r_ref.shape[2]
    nquad = dim // QUAD

    @pl.when(pl.program_id(1) == 0)
    def _():
        carry_scr[...] = jnp.zeros(carry_scr.shape, F32)
        carryl_scr[...] = jnp.zeros(carryl_scr.shape, F32)
        h_scr[...] = jnp.zeros(h_scr.shape, F32)

    ones_bd = ones_ref[...]

    def head_sum(x, parts):
        return jnp.concatenate(
            [_mm(x[:, q * QUAD:(q + 1) * QUAD], ones_bd, parts, 1) for q in range(nquad)], axis=1)

    def prepare(bb):
        def lerp(idx, ref, mu_row):
            x = ref[bb]
            prev = _token_shift(x, carry_scr[bb, idx, 0:1, :])
            carry_scr[bb, idx, 0:1, :] = x[c - 1:c, :]
            return x + (prev - x) * mu_row

        r = lerp(0, r_ref, mu_ref[0:1, :])
        k = lerp(1, k_ref, mu_ref[1:2, :])
        v = lerp(2, v_ref, mu_ref[2:3, :])
        lo0 = lo_ref[bb]
        lo_prev = _token_shift(lo0, carryl_scr[bb, 0:1, :])
        carryl_scr[bb, 0:1, :] = lo0[c - 1:c, :]
        lo = lo0 + (lo_prev - lo0) * mul_ref[...]
        lane = lax.broadcasted_iota(I32, lo.shape, 1)
        lo_act = jnp.where(lane < LORA_DIM, jnp.tanh(lo), lo)
        pre = _mm(lo_act, wcat_ref[...], 2, 2)
        w_pre = pre[:, :dim] + vec_ref[0:1, :]
        a_pre = pre[:, dim:] + vec_ref[1:2, :]
        sp = jnp.maximum(-w_pre, 0.0) + jnp.log(1.0 + jnp.exp(-jnp.abs(w_pre)))
        logw = -jnp.exp(-sp - 0.5)
        a = _sigmoid(a_pre)
        kk = k * vec_ref[2:3, :]
        kk = kk / jnp.maximum(jnp.sqrt(head_sum(kk * kk, 2)), 1e-12)
        kmod = k * (1.0 + (a - 1.0) * vec_ref[3:4, :])
        kka = kk * a
        cum = _mm(tri_ref[...], logw, 1, 2)
        tot = cum[c - 1:c, :]
        e_neg = jnp.exp(-cum)
        e_rem = jnp.exp(tot - cum)
        return dict(r=r, v=v, kmod=kmod, at=-kk * jnp.exp(cum - logw), bt=kka * e_neg, kt=kmod * e_neg,
                    rt=r * jnp.exp(cum), bh=kka * e_rem, kh=kmod * e_rem, gam=jnp.exp(tot))

    prep = [prepare(bb) for bb in range(nb)]

    lane_h = lax.broadcasted_iota(I32, (c, QUAD), 1) // RWKV_HEAD
    n4 = HEADS_PER_QUAD * c
    row = lax.broadcasted_iota(I32, (n4, n4), 0)
    col = lax.broadcasted_iota(I32, (n4, n4), 1)
    strict = row > col
    incl = row >= col
    diag = row == col

    def stack(z):
        return jnp.concatenate([jnp.where(lane_h == h, z, 0.0) for h in range(HEADS_PER_QUAD)], axis=0)

    def unstack(zs):
        out = zs[0:c]
        for h in range(1, HEADS_PER_QUAD):
            out = out + zs[h * c:(h + 1) * c]
        return out

    chains = [(bb, q) for bb in range(nb) for q in range(nquad)]
    ids = range(len(chains))

    def part(name, i):
        bb, q = chains[i]
        return prep[bb][name][:, q * QUAD:(q + 1) * QUAD]

    sa = [stack(part("at", i)) for i in ids]
    sv = [stack(part("v", i)) for i in ids]
    aa = [_mm(jnp.concatenate([sa[i], stack(part("rt", i))], axis=0),
              jnp.concatenate([stack(part("bt", i)), stack(part("kt", i))], axis=0), trans_b=True)
          for i in ids]
    n_ab = [jnp.where(strict, x[0:n4, 0:n4], 0.0) for x in aa]
    a_ak = [jnp.where(strict, x[0:n4, n4:], 0.0) for x in aa]
    a_rb = [jnp.where(incl, x[n4:, 0:n4], 0.0) for x in aa]
    a_rk = [jnp.where(incl, x[n4:, n4:], 0.0) for x in aa]
    tm = [jnp.where(diag, 1.0, x) for x in n_ab]
    p = n_ab
    for _ in range(int(math.log2(c)) - 1):
        p = [_mm(x, x) for x in p]
        tm = [t + _mm(t, x) for t, x in zip(tm, p)]
    ta = [_mm(tm[i], sa[i]) for i in ids]
    av = [_mm(a_ak[i], sv[i]) for i in ids]
    tav = [_mm(tm[i], av[i]) for i in ids]
    bs_t = [stack(part("bh", i)).T for i in ids]
    ks_t = [stack(part("kh", i)).T for i in ids]
    m_low = [_mm(bs_t[i], ta[i]) for i in ids]
    g_new = [_mm(bs_t[i], tav[i]) + _mm(ks_t[i], sv[i]) for i in ids]
    qmat = [part("rt", i) + unstack(_mm(a_rb[i], ta[i])) for i in ids]
    y_in = [unstack(_mm(a_rb[i], tav[i]) + _mm(a_rk[i], sv[i])) for i in ids]
    for i in ids:
        bb, q = chains[i]
        h0 = h_scr[bb, q]
        gam_col = jnp.sum(jnp.where(diag, part("gam", i), 0.0), axis=1, keepdims=True)
        y_scr[bb, :, q * QUAD:(q + 1) * QUAD] = y_in[i] + _mm(qmat[i], h0)
        h_scr[bb, q] = gam_col * h0 + _mm(m_low[i], h0) + g_new[i]

    inv_n = 1.0 / RWKV_HEAD
    for bb in range(nb):
        y = y_scr[bb]
        mu = head_sum(y, 1) * inv_n
        yc = y - mu
        var = head_sum(yc * yc, 1) * inv_n
        yn = yc * lax.rsqrt(var + GN_EPS) * vec_ref[4:5, :] + vec_ref[5:6, :]
        bonus = head_sum(prep[bb]["r"] * prep[bb]["kmod"] * rk_ref[...], 1) * prep[bb]["v"]
        o_ref[bb] = ((yn + bonus) * _silu(g_ref[bb])).astype(o_ref.dtype)


def _rwkv_branch(zm, zs, mu_rkv, mu_lora, vec, w_up, a_up, r_k, cols):
    b, s, _ = zm.shape
    dim = mu_rkv.shape[1]
    c = RWKV_CHUNK
    cr, ck, cv, cg = cols
    zero = jnp.zeros((LORA_DIM, dim), F32)
    wcat = jnp.concatenate([jnp.concatenate([w_up, zero], axis=1),
                            jnp.concatenate([zero, a_up], axis=1)], axis=0)
    tri = jnp.asarray(np.tril(np.ones((c, c), np.float32)), BF16)
    hid = np.arange(QUAD) // RWKV_HEAD
    ones_bd = jnp.asarray((hid[:, None] == hid[None, :]).astype(np.float32), BF16)
    const = lambda shape: pl.BlockSpec(shape, lambda bi, i: (0,) * len(shape))
    nb = RWKV_BATCH_ROWS if b % RWKV_BATCH_ROWS == 0 else 1
    return pl.pallas_call(
        _rwkv_kernel,
        grid=(b // nb, s // c),
        in_specs=[pl.BlockSpec((nb, c, dim), lambda bi, i: (bi, i, cr)),
                  pl.BlockSpec((nb, c, dim), lambda bi, i: (bi, i, ck)),
                  pl.BlockSpec((nb, c, dim), lambda bi, i: (bi, i, cv)),
                  pl.BlockSpec((nb, c, dim), lambda bi, i: (bi, i, cg)),
                  pl.BlockSpec((nb, c, 2 * LORA_DIM), lambda bi, i: (bi, i, 0)),
                  const((3, dim)), const((1, 2 * LORA_DIM)), const((6, dim)), const((1, dim)),
                  const((2 * LORA_DIM, 2 * dim)), const((c, c)), const((QUAD, QUAD))],
        out_specs=pl.BlockSpec((nb, c, dim), lambda bi, i: (bi, i, 0)),
        out_shape=jax.ShapeDtypeStruct((b, s, dim), BF16),
        scratch_shapes=[pltpu.VMEM((nb, 3, 8, dim), F32),
                        pltpu.VMEM((nb, 8, 2 * LORA_DIM), F32),
                        pltpu.VMEM((nb, dim // QUAD, QUAD, QUAD), F32),
                        pltpu.VMEM((nb, c, dim), F32)],
        compiler_params=_cparams(("arbitrary", "arbitrary")),
        name="rwkv7_chunk",
    )(zm, zm, zm, zm, zs, mu_rkv, mu_lora.reshape(1, 2 * LORA_DIM), vec, r_k.reshape(1, dim),
      wcat, tri, ones_bd)


def _out_proj_kernel(*refs, n_in, final):
    y_refs = refs[:n_in]
    w_refs = refs[n_in:2 * n_in]
    x_ref, gate_ref = refs[2 * n_in], refs[2 * n_in + 1]
    o_ref = refs[-1]
    acc = jnp.dot(y_refs[0][0], w_refs[0][...], preferred_element_type=F32)
    for y_ref, w_ref in zip(y_refs[1:], w_refs[1:]):
        acc = acc + jnp.dot(y_ref[0], w_ref[...], preferred_element_type=F32)
    xn = x_ref[0] + gate_ref[0] * acc
    if final:
        fg_ref = refs[2 * n_in + 2]
        ms = jnp.mean(xn * xn, axis=-1, keepdims=True)
        xn = xn * lax.rsqrt(ms + EPS) * fg_ref[...]
    o_ref[0] = xn


def _out_proj(ys, ws, x, gate, final_g, tm):
    b, s, d = x.shape
    n_in = len(ys)
    final = final_g is not None
    in_specs = [pl.BlockSpec((1, tm, y.shape[2]), lambda bi, i: (bi, i, 0)) for y in ys]
    in_specs += [pl.BlockSpec(w.shape, lambda bi, i: (0, 0)) for w in ws]
    in_specs += [pl.BlockSpec((1, tm, d), lambda bi, i: (bi, i, 0)),
                 pl.BlockSpec((1, 1, d), lambda bi, i: (bi, 0, 0))]
    args = list(ys) + list(ws) + [x, gate.reshape(b, 1, d)]
    if final:
        in_specs.append(pl.BlockSpec((1, d), lambda bi, i: (0, 0)))
        args.append(final_g.reshape(1, d))
    return pl.pallas_call(
        functools.partial(_out_proj_kernel, n_in=n_in, final=final),
        grid=(b, s // tm),
        in_specs=in_specs,
        out_specs=pl.BlockSpec((1, tm, d), lambda bi, i: (bi, i, 0)),
        out_shape=jax.ShapeDtypeStruct((b, s, d), F32),
        compiler_params=_cparams(("arbitrary", "arbitrary")),
        name="out_proj_final" if final else "out_proj",
    )(*args)


def _rope128(x, cos, sin_signed):
    return x * cos + pltpu.roll(x, 64, axis=1) * sin_signed


def _rope64_pair(x, cos, sin_signed):
    lane = lax.broadcasted_iota(I32, x.shape, 1)
    first_half = (lane % IDX_HEAD_DIM) < (IDX_HEAD_DIM // 2)
    partner = jnp.where(first_half, pltpu.roll(x, 96, axis=1), pltpu.roll(x, 32, axis=1))
    return x * cos + partner * sin_signed


def _rope_kv_kernel(zs_ref, pos_ref, inv_ref, sgn_ref, k_ref, v_ref, ki_ref, tab_ref):
    posf = pos_ref[0].astype(F32)
    ang128 = posf * inv_ref[0:1, :]
    ang64 = posf * inv_ref[1:2, :]
    c128 = jnp.cos(ang128)
    s128 = jnp.sin(ang128) * sgn_ref[0:1, :]
    c64 = jnp.cos(ang64)
    s64 = jnp.sin(ang64) * sgn_ref[1:2, :]
    tab_ref[0, 0] = c128
    tab_ref[0, 1] = s128
    tab_ref[0, 2] = c64
    tab_ref[0, 3] = s64
    zs = zs_ref[0]
    k_ref[0, 0] = _rope128(zs[:, 0:128], c128, s128).T.astype(BF16)
    v_ref[0, :, 0:128] = zs[:, 128:256].astype(BF16)
    v_ref[0, :, 128:256] = jnp.ones((zs.shape[0], 128), BF16)
    ki = _rope64_pair(zs[:, 256:384], c64, s64)
    lane = lax.broadcasted_iota(I32, ki.shape, 1)
    ki_ref[0, 0] = jnp.where(lane < IDX_HEAD_DIM, ki, pltpu.roll(ki, 64, axis=1)).T.astype(BF16)


def _rope_kv(zs, positions):
    b, s, ns = zs.shape
    t = 2 * KEY_TILE
    tile = pl.BlockSpec((1, 1, 128, t), lambda bi, i: (bi, i, 0, 0))
    inv128 = ROPE_THETA ** (-jnp.arange(0, ATT_HEAD_DIM, 2, dtype=F32) / ATT_HEAD_DIM)
    inv64 = ROPE_THETA ** (-jnp.arange(0, IDX_HEAD_DIM, 2, dtype=F32) / IDX_HEAD_DIM)
    inv = jnp.stack([jnp.tile(inv128, 2), jnp.tile(inv64, 4)])
    sgn128 = np.where(np.arange(128) < 64, -1.0, 1.0)
    sgn64 = np.where((np.arange(128) % 64) < 32, -1.0, 1.0)
    sgn = jnp.asarray(np.stack([sgn128, sgn64]), F32)
    blk = lambda w: pl.BlockSpec((1, t, w), lambda bi, i: (bi, i, 0))
    return pl.pallas_call(
        _rope_kv_kernel,
        grid=(b, s // t),
        in_specs=[blk(ns), blk(1),
                  pl.BlockSpec((2, 128), lambda bi, i: (0, 0)),
                  pl.BlockSpec((2, 128), lambda bi, i: (0, 0))],
        out_specs=[tile, blk(256), tile,
                   pl.BlockSpec((1, 4, t, 128), lambda bi, i: (bi, 0, i, 0))],
        out_shape=[jax.ShapeDtypeStruct((b, s // t, 128, t), BF16),
                   jax.ShapeDtypeStruct((b, s, 256), BF16),
                   jax.ShapeDtypeStruct((b, s // t, 128, t), BF16),
                   jax.ShapeDtypeStruct((b, 4, s, 128), F32)],
        compiler_params=_cparams(("arbitrary", "arbitrary")),
        name="rope_kv",
    )(zs, positions.reshape(b, s, 1), inv, sgn)


def _dsa_kernel(q_ref, g_ref, qi_ref, zs_ref, tab_ref, qpos_ref, kpos_ref, k_ref, v_ref, ki_ref,
                upper_ref, o_ref, qs_scr, qis_scr, wb_scr, key_scr, hi_scr, lo_scr, acc_scr, m_scr,
                *, k_sel):
    tq = q_ref.shape[1]
    qb = pl.program_id(1)
    nk = (qb * tq + tq + KEY_TILE - 1) // KEY_TILE
    npair = (nk + 1) // 2
    att_tile = 2 * KEY_TILE

    c128, s128 = tab_ref[0, 0], tab_ref[0, 1]
    c64, s64 = tab_ref[0, 2], tab_ref[0, 3]
    q_scale = (ATT_HEAD_DIM ** -0.5) * math.log2(math.e)
    lane = lax.broadcasted_iota(I32, (tq, 128), 1)
    eye = jnp.where(lax.broadcasted_iota(I32, (tq, 128), 0) == lane, 1.0, 0.0).astype(BF16)
    for h in range(ATT_HEADS):
        qh = _rope128(q_ref[0, :, h * 128:(h + 1) * 128], c128, s128) * q_scale
        qs_scr[h * tq:(h + 1) * tq, 0:128] = qh.astype(BF16)
        qs_scr[h * tq:(h + 1) * tq, 128:256] = eye
    zs = zs_ref[0]
    w_scale = (IDX_HEADS ** -0.5) * (IDX_HEAD_DIM ** -0.5)
    w_off = 256 + IDX_HEAD_DIM
    for p in range(IDX_HEADS // 2):
        pair = _rope64_pair(qi_ref[0, :, p * 128:(p + 1) * 128], c64, s64)
        qis_scr[(2 * p) * tq:(2 * p + 1) * tq, :] = jnp.where(lane < 64, pair, 0.0).astype(BF16)
        qis_scr[(2 * p + 1) * tq:(2 * p + 2) * tq, :] = jnp.where(lane >= 64, pair, 0.0).astype(BF16)
    for h in range(IDX_HEADS):
        wcol = zs[:, w_off + h:w_off + h + 1] * w_scale
        wb_scr[h * tq:(h + 1) * tq, :] = jnp.broadcast_to(wcol, (tq, 128))

    qchunk = qpos_ref[0] >> 6

    def score_body(jp, carry):
        kt = ki_ref[0, jp]
        nchunk = IDX_HEADS // 2

        def logits(c):
            return _mm(qis_scr[2 * c * tq:(2 * c + 2) * tq, :], kt)

        def weighted(c, sc):
            out = None
            for h in range(2):
                wb = jnp.concatenate([wb_scr[(2 * c + h) * tq:(2 * c + h + 1) * tq, :]] * (att_tile // 128),
                                     axis=1)
                term = jnp.maximum(sc[h * tq:(h + 1) * tq], 0.0) * wb
                out = term if out is None else out + term
            return out

        ahead = 2
        sc = {c: logits(c) for c in range(min(ahead, nchunk))}
        tot = None
        for c in range(nchunk):
            if c + ahead < nchunk:
                sc[c + ahead] = logits(c + ahead)
            w = weighted(c, sc.pop(c))
            tot = w if tot is None else tot + w
        for t in range(2):
            kchunk = kpos_ref[0, pl.ds(2 * jp + t, 1), :] >> 6
            part = jnp.where(kchunk <= qchunk, tot[:, t * KEY_TILE:(t + 1) * KEY_TILE], -jnp.inf)
            bits = pltpu.bitcast(part, I32)
            key = bits ^ ((bits >> 31) & 0x7FFFFFFF)
            key_scr[2 * jp + t] = key
            hi_scr[2 * jp + t] = (key >> 16).astype(I16)
        return carry

    lax.fori_loop(0, npair, score_body, 0)
    neg_key = INT_MIN + 0x7FFFFF

    one16 = jnp.ones((tq, KEY_TILE), I16)
    zero16 = jnp.zeros((tq, KEY_TILE), I16)

    def to16(col):
        return jnp.broadcast_to(col, (tq, KEY_TILE)).astype(I16)

    def count16(plane, cand16, strict):
        cmp = (lambda x: x > cand16) if strict else (lambda x: x >= cand16)

        def body(jp, acc):
            acc = acc + jnp.where(cmp(plane[2 * jp]), one16, zero16)
            return acc + jnp.where(cmp(plane[2 * jp + 1]), one16, zero16)

        acc = lax.fori_loop(0, npair, body, zero16)
        return jnp.sum(acc.astype(I32), axis=1, keepdims=True)

    def kth_largest16(plane, k_need):
        def bit_body(i, thr):
            cand = thr + jnp.left_shift(jnp.int32(1), 15 - i)
            cnt = count16(plane, to16(cand), False)
            return jnp.where(cnt >= k_need, cand, thr)
        return lax.fori_loop(0, 16, bit_body, jnp.full((tq, 1), MIN16, I32))

    thi = kth_largest16(hi_scr, k_sel)
    n_above = count16(hi_scr, to16(thi), True)

    def lo_body(j, carry):
        key = key_scr[j]
        lo = (key & 0xFFFF) - 32768
        lo_scr[j] = jnp.where((key >> 16) == thi, lo, MIN16).astype(I16)
        return carry

    lax.fori_loop(0, 2 * npair, lo_body, 0)
    tlo = kth_largest16(lo_scr, k_sel - n_above)
    n_gt = n_above + count16(lo_scr, to16(tlo), True)
    thr = thi * 65536 + (tlo + 32768)
    need = (k_sel - n_gt).astype(F32)

    acc_scr[...] = jnp.zeros(acc_scr.shape, F32)
    m_scr[...] = jnp.full(m_scr.shape, NEG_BIG, F32)
    upper = upper_ref[...]
    hpc = ATT_CHUNK_HEADS
    nchunk = ATT_HEADS // hpc
    crow = hpc * tq

    def mask_bias(step, eqc):
        st = jnp.minimum(step, npair - 1)
        parts = []
        for t in range(2):
            key = key_scr[2 * st + t]
            eqf = jnp.where(key == thr, 1.0, 0.0)
            before = _mm(eqf.astype(BF16), upper) + eqc
            take = jnp.where(key > thr, 1.0, jnp.where(before < need, eqf, 0.0))
            parts.append(jnp.where((take > 0.0) & (key > neg_key), 0.0, NEG_BIG).astype(BF16))
            eqc = eqc + jnp.sum(eqf, axis=1, keepdims=True)
        return jnp.concatenate(parts, axis=1), eqc

    def att_body(jp, carry):
        eqc, bias = carry
        rhs = jnp.concatenate([k_ref[0, jp], bias], axis=0)
        vt = v_ref[0, pl.ds(pl.multiple_of(jp * att_tile, att_tile), att_tile), :]

        def logits(c):
            return _mm(qs_scr[c * crow:(c + 1) * crow, :], rhs)

        def softmax(c, s):
            ps, alphas = [], []
            for h in range(hpc):
                rs = slice((hpc * c + h) * tq, (hpc * c + h + 1) * tq)
                sh = s[h * tq:(h + 1) * tq]
                m_old = m_scr[rs, :]
                m_new = jnp.maximum(m_old, jnp.max(sh, axis=1, keepdims=True))
                ps.append(jnp.exp2(sh - jnp.concatenate([m_new] * (att_tile // 128), axis=1)).astype(BF16))
                alphas.append(jnp.exp2(m_old - m_new))
                m_scr[rs, :] = m_new
            return jnp.concatenate(ps, axis=0), jnp.concatenate(alphas, axis=0)

        def accumulate(c, p, alpha):
            rs = slice(c * crow, (c + 1) * crow)
            acc_scr[rs, :] = jnp.concatenate([alpha, alpha], axis=1) * acc_scr[rs, :] + _mm(p, vt)

        ahead = 2
        s = {c: logits(c) for c in range(min(ahead, nchunk))}
        prev = None
        for c in range(nchunk):
            if c + ahead < nchunk:
                s[c + ahead] = logits(c + ahead)
            cur = softmax(c, s.pop(c))
            if prev is not None:
                accumulate(c - 1, *prev)
            if c == 0:
                bias_next, eqc = mask_bias(jp + 1, eqc)
            prev = cur
        accumulate(nchunk - 1, *prev)
        return eqc, bias_next

    bias0, eqc0 = mask_bias(0, jnp.zeros((tq, 1), F32))
    lax.fori_loop(0, npair, att_body, (eqc0, bias0))
    o = acc_scr[:, 0:128] / acc_scr[:, 128:256]
    for h in range(ATT_HEADS):
        sl = slice(h * 128, (h + 1) * 128)
        o_ref[0, :, sl] = (o[h * tq:(h + 1) * tq, :] * _silu(g_ref[0, :, sl])).astype(o_ref.dtype)


def _dsa(zm, zs, tabs, positions, k_rot, v_b, ki2):
    b, s, _ = zm.shape
    tq = Q_BLOCK
    att = ATT_HEADS * ATT_HEAD_DIM
    idx = IDX_HEADS * IDX_HEAD_DIM
    k_sel = min(TOPK_MAX, s // 4)
    nkt = s // KEY_TILE
    assert nkt % 2 == 0, "key tiles are consumed in pairs"
    rows = ATT_HEADS * tq
    upper = jnp.asarray(np.triu(np.ones((KEY_TILE, KEY_TILE), np.float32), 1), BF16)
    tiles = pl.BlockSpec((1, nkt // 2, 128, 2 * KEY_TILE), lambda bi, i: (bi, 0, 0, 0))
    return pl.pallas_call(
        functools.partial(_dsa_kernel, k_sel=k_sel),
        grid=(b, s // tq),
        in_specs=[pl.BlockSpec((1, tq, att), lambda bi, i: (bi, i, 0)),
                  pl.BlockSpec((1, tq, att), lambda bi, i: (bi, i, 1)),
                  pl.BlockSpec((1, tq, idx), lambda bi, i: (bi, i, 2 * att // idx)),
                  pl.BlockSpec((1, tq, zs.shape[2]), lambda bi, i: (bi, i, 0)),
                  pl.BlockSpec((1, 4, tq, 128), lambda bi, i: (bi, 0, i, 0)),
                  pl.BlockSpec((1, tq, 1), lambda bi, i: (bi, i, 0)),
                  pl.BlockSpec((1, nkt, KEY_TILE), lambda bi, i: (bi, 0, 0)),
                  tiles, pl.BlockSpec((1, s, 256), lambda bi, i: (bi, 0, 0)), tiles,
                  pl.BlockSpec((KEY_TILE, KEY_TILE), lambda bi, i: (0, 0))],
        out_specs=pl.BlockSpec((1, tq, att), lambda bi, i: (bi, i, 0)),
        out_shape=jax.ShapeDtypeStruct((b, s, att), BF16),
        scratch_shapes=[pltpu.VMEM((rows, 256), BF16),
                        pltpu.VMEM((IDX_HEADS * tq, 128), BF16),
                        pltpu.VMEM((IDX_HEADS * tq, 128), F32),
                        pltpu.VMEM((nkt, tq, KEY_TILE), I32),
                        pltpu.VMEM((nkt, tq, KEY_TILE), I16),
                        pltpu.VMEM((nkt, tq, KEY_TILE), I16),
                        pltpu.VMEM((rows, 256), F32),
                        pltpu.VMEM((rows, 128), F32)],
        compiler_params=_cparams(("arbitrary", "arbitrary")),
        name="dsa_attention",
    )(zm, zm, zm, zs, tabs, positions.reshape(b, s, 1), positions.reshape(b, nkt, KEY_TILE),
      k_rot, v_b, ki2, upper)


def _even_layer(x, scale, shift, gate, g, w_in, w_out, conv_w, conv_vec, mu_rkv, mu_lora, vec,
                w_up, a_up, r_k, final_g):
    d = x.shape[2]
    n6 = 6 * d
    lora = 2 * LORA_DIM
    w_main = jnp.concatenate([w_in[:, :n6], w_in[:, n6 + lora:]], axis=1).astype(BF16)
    w_small = w_in[:, n6:n6 + lora].astype(BF16)
    zm, zs = _norm_proj(x, g, scale, shift, w_main, w_small, tm=min(PROJ_ROWS, x.shape[1]), tn=1792)
    ya = _conv_branch(zm, conv_w[:, 0, :], conv_vec, cols=(0, 1, 2), t=min(CONV_ROWS, x.shape[1]))
    yb = _rwkv_branch(zm, zs, mu_rkv, mu_lora, vec, w_up, a_up, r_k, cols=(3, 4, 5, 6))
    wo = w_out.astype(BF16)
    return _out_proj([ya, yb], [wo[:d], wo[d:]], x, gate, final_g, tm=min(OUT_ROWS, x.shape[1]))


def _odd_layer(x, scale, shift, gate, g, w_in, w_out, positions, final_g):
    att = ATT_HEADS * ATT_HEAD_DIM
    idx = IDX_HEADS * IDX_HEAD_DIM
    o_q, o_k, o_v = 0, att, att + 128
    o_qi = o_v + 128
    o_ki = o_qi + idx
    o_wi = o_ki + IDX_HEAD_DIM
    o_g = o_wi + IDX_HEADS
    w_main = jnp.concatenate([w_in[:, o_q:o_k], w_in[:, o_g:], w_in[:, o_qi:o_ki]], axis=1).astype(BF16)
    pad = jnp.zeros((w_in.shape[0], 128 - IDX_HEAD_DIM - IDX_HEADS), w_in.dtype)
    w_small = jnp.concatenate([w_in[:, o_k:o_qi], w_in[:, o_ki:o_g], pad], axis=1).astype(BF16)
    zm, zs = _norm_proj(x, g, scale, shift, w_main, w_small, tm=min(PROJ_ROWS, x.shape[1]), tn=2304)
    k_rot, v_b, ki2, tabs = _rope_kv(zs, positions)
    o = _dsa(zm, zs, tabs, positions, k_rot, v_b, ki2)
    return _out_proj([o], [w_out.astype(BF16)], x, gate, final_g, tm=min(OUT_ROWS, x.shape[1]))


def kernel(x, c, positions, ada_w, ada_b, norm_g, final_g, even_w_in, even_w_out, conv_w, conv_vec,
           rwkv_mu_rkv, rwkv_mu_lora, rwkv_vec, rwkv_w_up, rwkv_a_up, rwkv_r_k, odd_w_in, odd_w_out):
    depth = ada_w.shape[0]
    d = x.shape[2]
    mod = _modulation(c, ada_w, ada_b)
    for l in range(depth):
        shift, scale, gate = mod[l, :, :d], mod[l, :, d:2 * d], mod[l, :, 2 * d:]
        fg = final_g if l == depth - 1 else None
        j = l // 2
        if l % 2 == 0:
            x = _even_layer(x, scale, shift, gate, norm_g[l], even_w_in[j], even_w_out[j], conv_w[j],
                            conv_vec[j], rwkv_mu_rkv[j], rwkv_mu_lora[j], rwkv_vec[j], rwkv_w_up[j],
                            rwkv_a_up[j], rwkv_r_k[j], fg)
        else:
            x = _odd_layer(x, scale, shift, gate, norm_g[l], odd_w_in[j], odd_w_out[j], positions, fg)
    return x
```

```python
import functools
import math

import jax
import jax.numpy as jnp
import numpy as np
from jax import lax
from jax.experimental import pallas as pl
from jax.experimental.pallas import tpu as pltpu

F32 = jnp.float32
BF16 = jnp.bfloat16
I32 = jnp.int32
I16 = jnp.int16

CHUNK = 64
EPS = 1e-6
LN_EPS = 1e-5
ROPE_THETA = 10000.0
CONV_WIDTH = 31
CONV_HALO = 32
SUBLANES = 8
LANES = 128
CONV_ACC_ROWS = 128
ATT_CHUNK_HEADS = 8
RWKV_BATCH_ROWS = 2
RWKV_HEAD = 64
LORA_DIM = 64
GN_EPS = 64e-5
ATT_HEADS = 16
ATT_HEAD_DIM = 128
IDX_HEADS = 8
IDX_HEAD_DIM = 64
TOPK_MAX = 256
Q_BLOCK = 128
KEY_TILE = 256
QUAD = 256
HEADS_PER_QUAD = QUAD // RWKV_HEAD
RWKV_CHUNK = QUAD // HEADS_PER_QUAD
NEG_BIG = -1e30
INT_MIN = -(2 ** 31)
MIN16 = -(2 ** 15)
VMEM_LIMIT = 56 * 1024 * 1024
PROJ_ROWS = 1024
OUT_ROWS = 512
CONV_ROWS = 256


def _cparams(sem):
    return pltpu.CompilerParams(dimension_semantics=sem, vmem_limit_bytes=VMEM_LIMIT)


def _split_bf16(x, n):
    if x.dtype == BF16:
        return [x]
    parts = []
    rem = x
    for i in range(n):
        p = rem.astype(BF16)
        parts.append(p)
        if i + 1 < n:
            rem = rem - p.astype(F32)
    return parts


def _mm(a, b, pa=1, pb=1, trans_b=False):
    ap = _split_bf16(a, pa)
    bp = _split_bf16(b, pb)
    dn = (((1,), (1 if trans_b else 0,)), ((), ()))
    order = max(len(ap), len(bp))
    acc = None
    for i, x in enumerate(ap):
        for j, y in enumerate(bp):
            if i + j >= order:
                continue
            t = lax.dot_general(x, y, dn, preferred_element_type=F32)
            acc = t if acc is None else acc + t
    return acc


def _sigmoid(x):
    return 1.0 / (1.0 + jnp.exp(-x))


def _silu(x):
    return x * _sigmoid(x)


def _mod_kernel(c_ref, w_ref, b_ref, o_ref):
    cond = _silu(c_ref[...])
    o_ref[0] = _mm(cond, w_ref[0], 2, 2) + b_ref[0]


def _modulation(c, ada_w, ada_b):
    depth, d, n = ada_w.shape
    b = c.shape[0]
    tn = 1024
    return pl.pallas_call(
        _mod_kernel,
        grid=(depth, n // tn),
        in_specs=[pl.BlockSpec((b, d), lambda l, j: (0, 0)),
                  pl.BlockSpec((1, d, tn), lambda l, j: (l, 0, j)),
                  pl.BlockSpec((1, 1, tn), lambda l, j: (l, 0, j))],
        out_specs=pl.BlockSpec((1, b, tn), lambda l, j: (l, 0, j)),
        out_shape=jax.ShapeDtypeStruct((depth, b, n), F32),
        compiler_params=_cparams(("arbitrary", "arbitrary")),
        name="adaln_mod",
    )(c, ada_w, ada_b.reshape(depth, 1, n))


def _norm_proj_kernel(x_ref, g_ref, sc_ref, sh_ref, wm_ref, ws_ref, zm_ref, zs_ref, h_scr):
    @pl.when(pl.program_id(2) == 0)
    def _():
        x = x_ref[0]
        ms = jnp.mean(x * x, axis=-1, keepdims=True)
        y = x * lax.rsqrt(ms + EPS) * g_ref[...]
        h = (y * (1.0 + sc_ref[0]) + sh_ref[0]).astype(BF16)
        h_scr[...] = h
        zs_ref[0] = jnp.dot(h, ws_ref[...], preferred_element_type=F32)

    zm_ref[0] = jnp.dot(h_scr[...], wm_ref[...], preferred_element_type=F32)


def _norm_proj(x, g, scale, shift, w_main, w_small, tm, tn):
    b, s, d = x.shape
    nm = w_main.shape[1]
    ns = w_small.shape[1]
    return pl.pallas_call(
        _norm_proj_kernel,
        grid=(b, s // tm, nm // tn),
        in_specs=[pl.BlockSpec((1, tm, d), lambda bi, i, j: (bi, i, 0)),
                  pl.BlockSpec((1, d), lambda bi, i, j: (0, 0)),
                  pl.BlockSpec((1, 1, d), lambda bi, i, j: (bi, 0, 0)),
                  pl.BlockSpec((1, 1, d), lambda bi, i, j: (bi, 0, 0)),
                  pl.BlockSpec((d, tn), lambda bi, i, j: (0, j)),
                  pl.BlockSpec((d, ns), lambda bi, i, j: (0, 0))],
        out_specs=[pl.BlockSpec((1, tm, tn), lambda bi, i, j: (bi, i, j)),
                   pl.BlockSpec((1, tm, ns), lambda bi, i, j: (bi, i, 0))],
        out_shape=[jax.ShapeDtypeStruct((b, s, nm), F32),
                   jax.ShapeDtypeStruct((b, s, ns), F32)],
        scratch_shapes=[pltpu.VMEM((tm, d), BF16)],
        compiler_params=_cparams(("arbitrary", "arbitrary", "arbitrary")),
        name="norm_proj",
    )(x, g.reshape(1, d), scale.reshape(b, 1, d), shift.reshape(b, 1, d), w_main, w_small)


def _conv_kernel(val_ref, glu_ref, gate_ref, w_ref, vec_ref, o_ref, u_scr, sh_scr, y_scr):
    t = val_ref.shape[1]
    c = val_ref.shape[2]

    @pl.when(pl.program_id(1) == 0)
    def _():
        u_scr[0:CONV_HALO, :] = jnp.zeros((CONV_HALO, u_scr.shape[1]), F32)

    @pl.when(pl.program_id(1) > 0)
    def _():
        u_scr[0:CONV_HALO, :] = u_scr[t:t + CONV_HALO, :]

    u_scr[CONV_HALO:CONV_HALO + t, :] = val_ref[0] * _sigmoid(glu_ref[0])
    ext = sh_scr.shape[1]
    for a in range(1, SUBLANES):
        sh_scr[a - 1] = u_scr[pl.ds(a, ext), :]
    first = CONV_HALO - (CONV_WIDTH - 1)
    rb = CONV_ACC_ROWS
    for cs in range(c // LANES):
        ls = slice(cs * LANES, (cs + 1) * LANES)
        for r0 in range(0, t, rb):
            acc = None
            for j in range(CONV_WIDTH):
                a, k8 = (first + j) % SUBLANES, (first + j) // SUBLANES * SUBLANES
                rows = slice(k8 + r0, k8 + r0 + rb)
                blk = u_scr[rows, ls] if a == 0 else sh_scr[a - 1, rows, ls]
                term = w_ref[j:j + 1, ls] * blk
                acc = term if acc is None else acc + term
            y_scr[r0:r0 + rb, ls] = acc + vec_ref[0:1, ls]
    y = y_scr[...]
    mu = jnp.mean(y, axis=-1, keepdims=True)
    yc = y - mu
    var = jnp.mean(yc * yc, axis=-1, keepdims=True)
    yn = yc * lax.rsqrt(var + LN_EPS) * vec_ref[1:2, :] + vec_ref[2:3, :]
    o_ref[0] = (_silu(yn) * _silu(gate_ref[0])).astype(o_ref.dtype)


def _conv_branch(zm, conv_w, conv_vec, cols, t):
    b, s, _ = zm.shape
    c = conv_w.shape[1]
    cv, cg, cs = cols
    return pl.pallas_call(
        _conv_kernel,
        grid=(b, s // t),
        in_specs=[pl.BlockSpec((1, t, c), lambda bi, i: (bi, i, cv)),
                  pl.BlockSpec((1, t, c), lambda bi, i: (bi, i, cg)),
                  pl.BlockSpec((1, t, c), lambda bi, i: (bi, i, cs)),
                  pl.BlockSpec((CONV_WIDTH, c), lambda bi, i: (0, 0)),
                  pl.BlockSpec((3, c), lambda bi, i: (0, 0))],
        out_specs=pl.BlockSpec((1, t, c), lambda bi, i: (bi, i, 0)),
        out_shape=jax.ShapeDtypeStruct((b, s, c), BF16),
        scratch_shapes=[pltpu.VMEM((CONV_HALO + t, c), F32),
                        pltpu.VMEM((SUBLANES - 1, t + CONV_HALO - SUBLANES, c), F32),
                        pltpu.VMEM((t, c), F32)],
        compiler_params=_cparams(("arbitrary", "arbitrary")),
        name="conformer_conv",
    )(zm, zm, zm, conv_w, conv_vec)


def _token_shift(x, carry_row):
    rolled = pltpu.roll(x, 1, axis=0)
    row = lax.broadcasted_iota(I32, x.shape, 0)
    return jnp.where(row == 0, carry_row, rolled)


def _rwkv_kernel(r_ref, k_ref, v_ref, g_ref, lo_ref, mu_ref, mul_ref, vec_ref, rk_ref, wcat_ref,
                 tri_ref, ones_ref, o_ref, carry_scr, carryl_scr, h_scr, y_scr):
    nb = r_ref.shape[0]
    c = r_ref.shape[1]
    dim = r_ref.shape[2]
    nquad = dim // QUAD

    @pl.when(pl.program_id(1) == 0)
    def _():
        carry_scr[...] = jnp.zeros(carry_scr.shape, F32)
        carryl_scr[...] = jnp.zeros(carryl_scr.shape, F32)
        h_scr[...] = jnp.zeros(h_scr.shape, F32)

    ones_bd = ones_ref[...]

    def head_sum(x, parts):
        return jnp.concatenate(
            [_mm(x[:, q * QUAD:(q + 1) * QUAD], ones_bd, parts, 1) for q in range(nquad)], axis=1)

    def prepare(bb):
        def lerp(idx, ref, mu_row):
            x = ref[bb]
            prev = _token_shift(x, carry_scr[bb, idx, 0:1, :])
            carry_scr[bb, idx, 0:1, :] = x[c - 1:c, :]
            return x + (prev - x) * mu_row

        r = lerp(0, r_ref, mu_ref[0:1, :])
        k = lerp(1, k_ref, mu_ref[1:2, :])
        v = lerp(2, v_ref, mu_ref[2:3, :])
        lo0 = lo_ref[bb]
        lo_prev = _token_shift(lo0, carryl_scr[bb, 0:1, :])
        carryl_scr[bb, 0:1, :] = lo0[c - 1:c, :]
        lo = lo0 + (lo_prev - lo0) * mul_ref[...]
        lane = lax.broadcasted_iota(I32, lo.shape, 1)
        lo_act = jnp.where(lane < LORA_DIM, jnp.tanh(lo), lo)
        pre = _mm(lo_act, wcat_ref[...], 2, 2)
        w_pre = pre[:, :dim] + vec_ref[0:1, :]
        a_pre = pre[:, dim:] + vec_ref[1:2, :]
        sp = jnp.maximum(-w_pre, 0.0) + jnp.log(1.0 + jnp.exp(-jnp.abs(w_pre)))
        logw = -jnp.exp(-sp - 0.5)
        a = _sigmoid(a_pre)
        kk = k * vec_ref[2:3, :]
        kk = kk / jnp.maximum(jnp.sqrt(head_sum(kk * kk, 2)), 1e-12)
        kmod = k * (1.0 + (a - 1.0) * vec_ref[3:4, :])
        kka = kk * a
        cum = _mm(tri_ref[...], logw, 1, 2)
        tot = cum[c - 1:c, :]
        e_neg = jnp.exp(-cum)
        e_rem = jnp.exp(tot - cum)
        return dict(r=r, v=v, kmod=kmod, at=-kk * jnp.exp(cum - logw), bt=kka * e_neg, kt=kmod * e_neg,
                    rt=r * jnp.exp(cum), bh=kka * e_rem, kh=kmod * e_rem, gam=jnp.exp(tot))

    prep = [prepare(bb) for bb in range(nb)]

    lane_h = lax.broadcasted_iota(I32, (c, QUAD), 1) // RWKV_HEAD
    n4 = HEADS_PER_QUAD * c
    row = lax.broadcasted_iota(I32, (n4, n4), 0)
    col = lax.broadcasted_iota(I32, (n4, n4), 1)
    strict = row > col
    incl = row >= col
    diag = row == col

    def stack(z):
        return jnp.concatenate([jnp.where(lane_h == h, z, 0.0) for h in range(HEADS_PER_QUAD)], axis=0)

    def unstack(zs):
        out = zs[0:c]
        for h in range(1, HEADS_PER_QUAD):
            out = out + zs[h * c:(h + 1) * c]
        return out

    chains = [(bb, q) for bb in range(nb) for q in range(nquad)]
    ids = range(len(chains))

    def part(name, i):
        bb, q = chains[i]
        return prep[bb][name][:, q * QUAD:(q + 1) * QUAD]

    sa = [stack(part("at", i)) for i in ids]
    sv = [stack(part("v", i)) for i in ids]
    aa = [_mm(jnp.concatenate([sa[i], stack(part("rt", i))], axis=0),
              jnp.concatenate([stack(part("bt", i)), stack(part("kt", i))], axis=0), trans_b=True)
          for i in ids]
    n_ab = [jnp.where(strict, x[0:n4, 0:n4], 0.0) for x in aa]
    a_ak = [jnp.where(strict, x[0:n4, n4:], 0.0) for x in aa]
    a_rb = [jnp.where(incl, x[n4:, 0:n4], 0.0) for x in aa]
    a_rk = [jnp.where(incl, x[n4:, n4:], 0.0) for x in aa]
    tm = [jnp.where(diag, 1.0, x) for x in n_ab]
    p = n_ab
    for _ in range(int(math.log2(c)) - 1):
        p = [_mm(x, x) for x in p]
        tm = [t + _mm(t, x) for t, x in zip(tm, p)]
    ta = [_mm(tm[i], sa[i]) for i in ids]
    av = [_mm(a_ak[i], sv[i]) for i in ids]
    tav = [_mm(tm[i], av[i]) for i in ids]
    bs_t = [stack(part("bh", i)).T for i in ids]
    ks_t = [stack(part("kh", i)).T for i in ids]
    m_low = [_mm(bs_t[i], ta[i]) for i in ids]
    g_new = [_mm(bs_t[i], tav[i]) + _mm(ks_t[i], sv[i]) for i in ids]
    qmat = [part("rt", i) + unstack(_mm(a_rb[i], ta[i])) for i in ids]
    y_in = [unstack(_mm(a_rb[i], tav[i]) + _mm(a_rk[i], sv[i])) for i in ids]
    for i in ids:
        bb, q = chains[i]
        h0 = h_scr[bb, q]
        gam_col = jnp.sum(jnp.where(diag, part("gam", i), 0.0), axis=1, keepdims=True)
        y_scr[bb, :, q * QUAD:(q + 1) * QUAD] = y_in[i] + _mm(qmat[i], h0)
        h_scr[bb, q] = gam_col * h0 + _mm(m_low[i], h0) + g_new[i]

    inv_n = 1.0 / RWKV_HEAD
    for bb in range(nb):
        y = y_scr[bb]
        mu = head_sum(y, 1) * inv_n
        yc = y - mu
        var = head_sum(yc * yc, 1) * inv_n
        yn = yc * lax.rsqrt(var + GN_EPS) * vec_ref[4:5, :] + vec_ref[5:6, :]
        bonus = head_sum(prep[bb]["r"] * prep[bb]["kmod"] * rk_ref[...], 1) * prep[bb]["v"]
        o_ref[bb] = ((yn + bonus) * _silu(g_ref[bb])).astype(o_ref.dtype)


def _rwkv_branch(zm, zs, mu_rkv, mu_lora, vec, w_up, a_up, r_k, cols):
    b, s, _ = zm.shape
    dim = mu_rkv.shape[1]
    c = RWKV_CHUNK
    cr, ck, cv, cg = cols
    zero = jnp.zeros((LORA_DIM, dim), F32)
    wcat = jnp.concatenate([jnp.concatenate([w_up, zero], axis=1),
                            jnp.concatenate([zero, a_up], axis=1)], axis=0)
    tri = jnp.asarray(np.tril(np.ones((c, c), np.float32)), BF16)
    hid = np.arange(QUAD) // RWKV_HEAD
    ones_bd = jnp.asarray((hid[:, None] == hid[None, :]).astype(np.float32), BF16)
    const = lambda shape: pl.BlockSpec(shape, lambda bi, i: (0,) * len(shape))
    nb = RWKV_BATCH_ROWS if b % RWKV_BATCH_ROWS == 0 else 1
    return pl.pallas_call(
        _rwkv_kernel,
        grid=(b // nb, s // c),
        in_specs=[pl.BlockSpec((nb, c, dim), lambda bi, i: (bi, i, cr)),
                  pl.BlockSpec((nb, c, dim), lambda bi, i: (bi, i, ck)),
                  pl.BlockSpec((nb, c, dim), lambda bi, i: (bi, i, cv)),
                  pl.BlockSpec((nb, c, dim), lambda bi, i: (bi, i, cg)),
                  pl.BlockSpec((nb, c, 2 * LORA_DIM), lambda bi, i: (bi, i, 0)),
                  const((3, dim)), const((1, 2 * LORA_DIM)), const((6, dim)), const((1, dim)),
                  const((2 * LORA_DIM, 2 * dim)), const((c, c)), const((QUAD, QUAD))],
        out_specs=pl.BlockSpec((nb, c, dim), lambda bi, i: (bi, i, 0)),
        out_shape=jax.ShapeDtypeStruct((b, s, dim), BF16),
        scratch_shapes=[pltpu.VMEM((nb, 3, 8, dim), F32),
                        pltpu.VMEM((nb, 8, 2 * LORA_DIM), F32),
                        pltpu.VMEM((nb, dim // QUAD, QUAD, QUAD), F32),
                        pltpu.VMEM((nb, c, dim), F32)],
        compiler_params=_cparams(("arbitrary", "arbitrary")),
        name="rwkv7_chunk",
    )(zm, zm, zm, zm, zs, mu_rkv, mu_lora.reshape(1, 2 * LORA_DIM), vec, r_k.reshape(1, dim),
      wcat, tri, ones_bd)


def _out_proj_kernel(*refs, n_in, final):
    y_refs = refs[:n_in]
    w_refs = refs[n_in:2 * n_in]
    x_ref, gate_ref = refs[2 * n_in], refs[2 * n_in + 1]
    o_ref = refs[-1]
    acc = jnp.dot(y_refs[0][0], w_refs[0][...], preferred_element_type=F32)
    for y_ref, w_ref in zip(y_refs[1:], w_refs[1:]):
        acc = acc + jnp.dot(y_ref[0], w_ref[...], preferred_element_type=F32)
    xn = x_ref[0] + gate_ref[0] * acc
    if final:
        fg_ref = refs[2 * n_in + 2]
        ms = jnp.mean(xn * xn, axis=-1, keepdims=True)
        xn = xn * lax.rsqrt(ms + EPS) * fg_ref[...]
    o_ref[0] = xn


def _out_proj(ys, ws, x, gate, final_g, tm):
    b, s, d = x.shape
    n_in = len(ys)
    final = final_g is not None
    in_specs = [pl.BlockSpec((1, tm, y.shape[2]), lambda bi, i: (bi, i, 0)) for y in ys]
    in_specs += [pl.BlockSpec(w.shape, lambda bi, i: (0, 0)) for w in ws]
    in_specs += [pl.BlockSpec((1, tm, d), lambda bi, i: (bi, i, 0)),
                 pl.BlockSpec((1, 1, d), lambda bi, i: (bi, 0, 0))]
    args = list(ys) + list(ws) + [x, gate.reshape(b, 1, d)]
    if final:
        in_specs.append(pl.BlockSpec((1, d), lambda bi, i: (0, 0)))
        args.append(final_g.reshape(1, d))
    return pl.pallas_call(
        functools.partial(_out_proj_kernel, n_in=n_in, final=final),
        grid=(b, s // tm),
        in_specs=in_specs,
        out_specs=pl.BlockSpec((1, tm, d), lambda bi, i: (bi, i, 0)),
        out_shape=jax.ShapeDtypeStruct((b, s, d), F32),
        compiler_params=_cparams(("arbitrary", "arbitrary")),
        name="out_proj_final" if final else "out_proj",
    )(*args)


def _rope128(x, cos, sin_signed):
    return x * cos + pltpu.roll(x, 64, axis=1) * sin_signed


def _rope64_pair(x, cos, sin_signed):
    lane = lax.broadcasted_iota(I32, x.shape, 1)
    first_half = (lane % IDX_HEAD_DIM) < (IDX_HEAD_DIM // 2)
    partner = jnp.where(first_half, pltpu.roll(x, 96, axis=1), pltpu.roll(x, 32, axis=1))
    return x * cos + partner * sin_signed


def _rope_kv_kernel(zs_ref, pos_ref, inv_ref, sgn_ref, k_ref, v_ref, ki_ref, tab_ref):
    posf = pos_ref[0].astype(F32)
    ang128 = posf * inv_ref[0:1, :]
    ang64 = posf * inv_ref[1:2, :]
    c128 = jnp.cos(ang128)
    s128 = jnp.sin(ang128) * sgn_ref[0:1, :]
    c64 = jnp.cos(ang64)
    s64 = jnp.sin(ang64) * sgn_ref[1:2, :]
    tab_ref[0, 0] = c128
    tab_ref[0, 1] = s128
    tab_ref[0, 2] = c64
    tab_ref[0, 3] = s64
    zs = zs_ref[0]
    k_ref[0, 0] = _rope128(zs[:, 0:128], c128, s128).T.astype(BF16)
    v_ref[0, :, 0:128] = zs[:, 128:256].astype(BF16)
    v_ref[0, :, 128:256] = jnp.ones((zs.shape[0], 128), BF16)
    ki = _rope64_pair(zs[:, 256:384], c64, s64)
    lane = lax.broadcasted_iota(I32, ki.shape, 1)
    ki_ref[0, 0] = jnp.where(lane < IDX_HEAD_DIM, ki, pltpu.roll(ki, 64, axis=1)).T.astype(BF16)


def _rope_kv(zs, positions):
    b, s, ns = zs.shape
    t = 2 * KEY_TILE
    tile = pl.BlockSpec((1, 1, 128, t), lambda bi, i: (bi, i, 0, 0))
    inv128 = ROPE_THETA ** (-jnp.arange(0, ATT_HEAD_DIM, 2, dtype=F32) / ATT_HEAD_DIM)
    inv64 = ROPE_THETA ** (-jnp.arange(0, IDX_HEAD_DIM, 2, dtype=F32) / IDX_HEAD_DIM)
    inv = jnp.stack([jnp.tile(inv128, 2), jnp.tile(inv64, 4)])
    sgn128 = np.where(np.arange(128) < 64, -1.0, 1.0)
    sgn64 = np.where((np.arange(128) % 64) < 32, -1.0, 1.0)
    sgn = jnp.asarray(np.stack([sgn128, sgn64]), F32)
    blk = lambda w: pl.BlockSpec((1, t, w), lambda bi, i: (bi, i, 0))
    return pl.pallas_call(
        _rope_kv_kernel,
        grid=(b, s // t),
        in_specs=[blk(ns), blk(1),
                  pl.BlockSpec((2, 128), lambda bi, i: (0, 0)),
                  pl.BlockSpec((2, 128), lambda bi, i: (0, 0))],
        out_specs=[tile, blk(256), tile,
                   pl.BlockSpec((1, 4, t, 128), lambda bi, i: (bi, 0, i, 0))],
        out_shape=[jax.ShapeDtypeStruct((b, s // t, 128, t), BF16),
                   jax.ShapeDtypeStruct((b, s, 256), BF16),
                   jax.ShapeDtypeStruct((b, s // t, 128, t), BF16),
                   jax.ShapeDtypeStruct((b, 4, s, 128), F32)],
        compiler_params=_cparams(("arbitrary", "arbitrary")),
        name="rope_kv",
    )(zs, positions.reshape(b, s, 1), inv, sgn)


def _dsa_kernel(q_ref, g_ref, qi_ref, zs_ref, tab_ref, qpos_ref, kpos_ref, k_ref, v_ref, ki_ref,
                upper_ref, o_ref, qs_scr, qis_scr, wb_scr, key_scr, hi_scr, lo_scr, acc_scr, m_scr,
                *, k_sel):
    tq = q_ref.shape[1]
    qb = pl.program_id(1)
    nk = (qb * tq + tq + KEY_TILE - 1) // KEY_TILE
    npair = (nk + 1) // 2
    att_tile = 2 * KEY_TILE

    c128, s128 = tab_ref[0, 0], tab_ref[0, 1]
    c64, s64 = tab_ref[0, 2], tab_ref[0, 3]
    q_scale = (ATT_HEAD_DIM ** -0.5) * math.log2(math.e)
    lane = lax.broadcasted_iota(I32, (tq, 128), 1)
    eye = jnp.where(lax.broadcasted_iota(I32, (tq, 128), 0) == lane, 1.0, 0.0).astype(BF16)
    for h in range(ATT_HEADS):
        qh = _rope128(q_ref[0, :, h * 128:(h + 1) * 128], c128, s128) * q_scale
        qs_scr[h * tq:(h + 1) * tq, 0:128] = qh.astype(BF16)
        qs_scr[h * tq:(h + 1) * tq, 128:256] = eye
    zs = zs_ref[0]
    w_scale = (IDX_HEADS ** -0.5) * (IDX_HEAD_DIM ** -0.5)
    w_off = 256 + IDX_HEAD_DIM
    for p in range(IDX_HEADS // 2):
        pair = _rope64_pair(qi_ref[0, :, p * 128:(p + 1) * 128], c64, s64)
        qis_scr[(2 * p) * tq:(2 * p + 1) * tq, :] = jnp.where(lane < 64, pair, 0.0).astype(BF16)
        qis_scr[(2 * p + 1) * tq:(2 * p + 2) * tq, :] = jnp.where(lane >= 64, pair, 0.0).astype(BF16)
    for h in range(IDX_HEADS):
        wcol = zs[:, w_off + h:w_off + h + 1] * w_scale
        wb_scr[h * tq:(h + 1) * tq, :] = jnp.broadcast_to(wcol, (tq, 128))

    qchunk = qpos_ref[0] >> 6

    def score_body(jp, carry):
        kt = ki_ref[0, jp]
        nchunk = IDX_HEADS // 2

        def logits(c):
            return _mm(qis_scr[2 * c * tq:(2 * c + 2) * tq, :], kt)

        def weighted(c, sc):
            out = None
            for h in range(2):
                wb = jnp.concatenate([wb_scr[(2 * c + h) * tq:(2 * c + h + 1) * tq, :]] * (att_tile // 128),
                                     axis=1)
                term = jnp.maximum(sc[h * tq:(h + 1) * tq], 0.0) * wb
                out = term if out is None else out + term
            return out

        ahead = 2
        sc = {c: logits(c) for c in range(min(ahead, nchunk))}
        tot = None
        for c in range(nchunk):
            if c + ahead < nchunk:
                sc[c + ahead] = logits(c + ahead)
            w = weighted(c, sc.pop(c))
            tot = w if tot is None else tot + w
        for t in range(2):
            kchunk = kpos_ref[0, pl.ds(2 * jp + t, 1), :] >> 6
            part = jnp.where(kchunk <= qchunk, tot[:, t * KEY_TILE:(t + 1) * KEY_TILE], -jnp.inf)
            bits = pltpu.bitcast(part, I32)
            key = bits ^ ((bits >> 31) & 0x7FFFFFFF)
            key_scr[2 * jp + t] = key
            hi_scr[2 * jp + t] = (key >> 16).astype(I16)
        return carry

    lax.fori_loop(0, npair, score_body, 0)
    neg_key = INT_MIN + 0x7FFFFF

    one16 = jnp.ones((tq, KEY_TILE), I16)
    zero16 = jnp.zeros((tq, KEY_TILE), I16)

    def to16(col):
        return jnp.broadcast_to(col, (tq, KEY_TILE)).astype(I16)

    def count16(plane, cand16, strict):
        cmp = (lambda x: x > cand16) if strict else (lambda x: x >= cand16)

        def body(jp, acc):
            acc = acc + jnp.where(cmp(plane[2 * jp]), one16, zero16)
            return acc + jnp.where(cmp(plane[2 * jp + 1]), one16, zero16)

        acc = lax.fori_loop(0, npair, body, zero16)
        return jnp.sum(acc.astype(I32), axis=1, keepdims=True)

    def kth_largest16(plane, k_need):
        def bit_body(i, thr):
            cand = thr + jnp.left_shift(jnp.int32(1), 15 - i)
            cnt = count16(plane, to16(cand), False)
            return jnp.where(cnt >= k_need, cand, thr)
        return lax.fori_loop(0, 16, bit_body, jnp.full((tq, 1), MIN16, I32))

    thi = kth_largest16(hi_scr, k_sel)
    n_above = count16(hi_scr, to16(thi), True)

    def lo_body(j, carry):
        key = key_scr[j]
        lo = (key & 0xFFFF) - 32768
        lo_scr[j] = jnp.where((key >> 16) == thi, lo, MIN16).astype(I16)
        return carry

    lax.fori_loop(0, 2 * npair, lo_body, 0)
    tlo = kth_largest16(lo_scr, k_sel - n_above)
    n_gt = n_above + count16(lo_scr, to16(tlo), True)
    thr = thi * 65536 + (tlo + 32768)
    need = (k_sel - n_gt).astype(F32)

    acc_scr[...] = jnp.zeros(acc_scr.shape, F32)
    m_scr[...] = jnp.full(m_scr.shape, NEG_BIG, F32)
    upper = upper_ref[...]
    hpc = ATT_CHUNK_HEADS
    nchunk = ATT_HEADS // hpc
    crow = hpc * tq

    def mask_bias(step, eqc):
        st = jnp.minimum(step, npair - 1)
        parts = []
        for t in range(2):
            key = key_scr[2 * st + t]
            eqf = jnp.where(key == thr, 1.0, 0.0)
            before = _mm(eqf.astype(BF16), upper) + eqc
            take = jnp.where(key > thr, 1.0, jnp.where(before < need, eqf, 0.0))
            parts.append(jnp.where((take > 0.0) & (key > neg_key), 0.0, NEG_BIG).astype(BF16))
            eqc = eqc + jnp.sum(eqf, axis=1, keepdims=True)
        return jnp.concatenate(parts, axis=1), eqc

    def att_body(jp, carry):
        eqc, bias = carry
        rhs = jnp.concatenate([k_ref[0, jp], bias], axis=0)
        vt = v_ref[0, pl.ds(pl.multiple_of(jp * att_tile, att_tile), att_tile), :]

        def logits(c):
            return _mm(qs_scr[c * crow:(c + 1) * crow, :], rhs)

        def softmax(c, s):
            ps, alphas = [], []
            for h in range(hpc):
                rs = slice((hpc * c + h) * tq, (hpc * c + h + 1) * tq)
                sh = s[h * tq:(h + 1) * tq].astype(BF16)
                m_old = m_scr[rs, :]
                tile_max = sh[:, 0:128]
                for t in range(1, att_tile // 128):
                    tile_max = jnp.maximum(tile_max, sh[:, t * 128:(t + 1) * 128])
                m_new = jnp.maximum(m_old, jnp.max(tile_max.astype(F32), axis=1, keepdims=True))
                ps.append(jnp.exp2(sh - jnp.concatenate([m_new.astype(BF16)] * (att_tile // 128), axis=1)))
                alphas.append(jnp.exp2(m_old - m_new))
                m_scr[rs, :] = m_new
            return jnp.concatenate(ps, axis=0), jnp.concatenate(alphas, axis=0)

        def accumulate(c, p, alpha):
            rs = slice(c * crow, (c + 1) * crow)
            acc_scr[rs, :] = jnp.concatenate([alpha, alpha], axis=1) * acc_scr[rs, :] + _mm(p, vt)

        ahead = 2
        s = {c: logits(c) for c in range(min(ahead, nchunk))}
        prev = None
        for c in range(nchunk):
            if c + ahead < nchunk:
                s[c + ahead] = logits(c + ahead)
            cur = softmax(c, s.pop(c))
            if prev is not None:
                accumulate(c - 1, *prev)
            if c == 0:
                bias_next, eqc = mask_bias(jp + 1, eqc)
            prev = cur
        accumulate(nchunk - 1, *prev)
        return eqc, bias_next

    bias0, eqc0 = mask_bias(0, jnp.zeros((tq, 1), F32))
    lax.fori_loop(0, npair, att_body, (eqc0, bias0))
    o = acc_scr[:, 0:128] / acc_scr[:, 128:256]
    for h in range(ATT_HEADS):
        sl = slice(h * 128, (h + 1) * 128)
        o_ref[0, :, sl] = (o[h * tq:(h + 1) * tq, :] * _silu(g_ref[0, :, sl])).astype(o_ref.dtype)


def _dsa(zm, zs, tabs, positions, k_rot, v_b, ki2):
    b, s, _ = zm.shape
    tq = Q_BLOCK
    att = ATT_HEADS * ATT_HEAD_DIM
    idx = IDX_HEADS * IDX_HEAD_DIM
    k_sel = min(TOPK_MAX, s // 4)
    nkt = s // KEY_TILE
    assert nkt % 2 == 0, "key tiles are consumed in pairs"
    rows = ATT_HEADS * tq
    upper = jnp.asarray(np.triu(np.ones((KEY_TILE, KEY_TILE), np.float32), 1), BF16)
    tiles = pl.BlockSpec((1, nkt // 2, 128, 2 * KEY_TILE), lambda bi, i: (bi, 0, 0, 0))
    return pl.pallas_call(
        functools.partial(_dsa_kernel, k_sel=k_sel),
        grid=(b, s // tq),
        in_specs=[pl.BlockSpec((1, tq, att), lambda bi, i: (bi, i, 0)),
                  pl.BlockSpec((1, tq, att), lambda bi, i: (bi, i, 1)),
                  pl.BlockSpec((1, tq, idx), lambda bi, i: (bi, i, 2 * att // idx)),
                  pl.BlockSpec((1, tq, zs.shape[2]), lambda bi, i: (bi, i, 0)),
                  pl.BlockSpec((1, 4, tq, 128), lambda bi, i: (bi, 0, i, 0)),
                  pl.BlockSpec((1, tq, 1), lambda bi, i: (bi, i, 0)),
                  pl.BlockSpec((1, nkt, KEY_TILE), lambda bi, i: (bi, 0, 0)),
                  tiles, pl.BlockSpec((1, s, 256), lambda bi, i: (bi, 0, 0)), tiles,
                  pl.BlockSpec((KEY_TILE, KEY_TILE), lambda bi, i: (0, 0))],
        out_specs=pl.BlockSpec((1, tq, att), lambda bi, i: (bi, i, 0)),
        out_shape=jax.ShapeDtypeStruct((b, s, att), BF16),
        scratch_shapes=[pltpu.VMEM((rows, 256), BF16),
                        pltpu.VMEM((IDX_HEADS * tq, 128), BF16),
                        pltpu.VMEM((IDX_HEADS * tq, 128), F32),
                        pltpu.VMEM((nkt, tq, KEY_TILE), I32),
                        pltpu.VMEM((nkt, tq, KEY_TILE), I16),
                        pltpu.VMEM((nkt, tq, KEY_TILE), I16),
                        pltpu.VMEM((rows, 256), F32),
                        pltpu.VMEM((rows, 128), F32)],
        compiler_params=_cparams(("arbitrary", "arbitrary")),
        name="dsa_attention",
    )(zm, zm, zm, zs, tabs, positions.reshape(b, s, 1), positions.reshape(b, nkt, KEY_TILE),
      k_rot, v_b, ki2, upper)


def _even_layer(x, scale, shift, gate, g, w_in, w_out, conv_w, conv_vec, mu_rkv, mu_lora, vec,
                w_up, a_up, r_k, final_g):
    d = x.shape[2]
    n6 = 6 * d
    lora = 2 * LORA_DIM
    w_main = jnp.concatenate([w_in[:, :n6], w_in[:, n6 + lora:]], axis=1).astype(BF16)
    w_small = w_in[:, n6:n6 + lora].astype(BF16)
    zm, zs = _norm_proj(x, g, scale, shift, w_main, w_small, tm=min(PROJ_ROWS, x.shape[1]), tn=1792)
    ya = _conv_branch(zm, conv_w[:, 0, :], conv_vec, cols=(0, 1, 2), t=min(CONV_ROWS, x.shape[1]))
    yb = _rwkv_branch(zm, zs, mu_rkv, mu_lora, vec, w_up, a_up, r_k, cols=(3, 4, 5, 6))
    wo = w_out.astype(BF16)
    return _out_proj([ya, yb], [wo[:d], wo[d:]], x, gate, final_g, tm=min(OUT_ROWS, x.shape[1]))


def _odd_layer(x, scale, shift, gate, g, w_in, w_out, positions, final_g):
    att = ATT_HEADS * ATT_HEAD_DIM
    idx = IDX_HEADS * IDX_HEAD_DIM
    o_q, o_k, o_v = 0, att, att + 128
    o_qi = o_v + 128
    o_ki = o_qi + idx
    o_wi = o_ki + IDX_HEAD_DIM
    o_g = o_wi + IDX_HEADS
    w_main = jnp.concatenate([w_in[:, o_q:o_k], w_in[:, o_g:], w_in[:, o_qi:o_ki]], axis=1).astype(BF16)
    pad = jnp.zeros((w_in.shape[0], 128 - IDX_HEAD_DIM - IDX_HEADS), w_in.dtype)
    w_small = jnp.concatenate([w_in[:, o_k:o_qi], w_in[:, o_ki:o_g], pad], axis=1).astype(BF16)
    zm, zs = _norm_proj(x, g, scale, shift, w_main, w_small, tm=min(PROJ_ROWS, x.shape[1]), tn=2304)
    k_rot, v_b, ki2, tabs = _rope_kv(zs, positions)
    o = _dsa(zm, zs, tabs, positions, k_rot, v_b, ki2)
    return _out_proj([o], [w_out.astype(BF16)], x, gate, final_g, tm=min(OUT_ROWS, x.shape[1]))


def kernel(x, c, positions, ada_w, ada_b, norm_g, final_g, even_w_in, even_w_out, conv_w, conv_vec,
           rwkv_mu_rkv, rwkv_mu_lora, rwkv_vec, rwkv_w_up, rwkv_a_up, rwkv_r_k, odd_w_in, odd_w_out):
    depth = ada_w.shape[0]
    d = x.shape[2]
    mod = _modulation(c, ada_w, ada_b)
    for l in range(depth):
        shift, scale, gate = mod[l, :, :d], mod[l, :, d:2 * d], mod[l, :, 2 * d:]
        fg = final_g if l == depth - 1 else None
        j = l // 2
        if l % 2 == 0:
            x = _even_layer(x, scale, shift, gate, norm_g[l], even_w_in[j], even_w_out[j], conv_w[j],
                            conv_vec[j], rwkv_mu_rkv[j], rwkv_mu_lora[j], rwkv_vec[j], rwkv_w_up[j],
                            rwkv_a_up[j], rwkv_r_k[j], fg)
        else:
            x = _odd_layer(x, scale, shift, gate, norm_g[l], odd_w_in[j], odd_w_out[j], positions, fg)
    return x
```

```python
import functools
import math

import jax
import jax.numpy as jnp
import numpy as np
from jax import lax
from jax.experimental import pallas as pl
from jax.experimental.pallas import tpu as pltpu

F32 = jnp.float32
BF16 = jnp.bfloat16
I32 = jnp.int32
I16 = jnp.int16

CHUNK = 64
EPS = 1e-6
LN_EPS = 1e-5
ROPE_THETA = 10000.0
CONV_WIDTH = 31
CONV_HALO = 32
SUBLANES = 8
LANES = 128
CONV_ACC_ROWS = 128
ATT_CHUNK_HEADS = 8
RWKV_BATCH_ROWS = 2
RWKV_HEAD = 64
LORA_DIM = 64
GN_EPS = 64e-5
ATT_HEADS = 16
ATT_HEAD_DIM = 128
IDX_HEADS = 8
IDX_HEAD_DIM = 64
TOPK_MAX = 256
Q_BLOCK = 128
KEY_TILE = 256
QUAD = 256
HEADS_PER_QUAD = QUAD // RWKV_HEAD
RWKV_CHUNK = QUAD // HEADS_PER_QUAD
NEG_BIG = -1e30
INT_MIN = -(2 ** 31)
MIN16 = -(2 ** 15)
VMEM_LIMIT = 56 * 1024 * 1024
PROJ_ROWS = 1024
OUT_ROWS = 512
CONV_SPREAD = 8


def _cparams(sem):
    return pltpu.CompilerParams(dimension_semantics=sem, vmem_limit_bytes=VMEM_LIMIT)


def _split_bf16(x, n):
    if x.dtype == BF16:
        return [x]
    parts = []
    rem = x
    for i in range(n):
        p = rem.astype(BF16)
        parts.append(p)
        if i + 1 < n:
            rem = rem - p.astype(F32)
    return parts


def _mm(a, b, pa=1, pb=1, trans_b=False):
    ap = _split_bf16(a, pa)
    bp = _split_bf16(b, pb)
    dn = (((1,), (1 if trans_b else 0,)), ((), ()))
    order = max(len(ap), len(bp))
    acc = None
    for i, x in enumerate(ap):
        for j, y in enumerate(bp):
            if i + j >= order:
                continue
            t = lax.dot_general(x, y, dn, preferred_element_type=F32)
            acc = t if acc is None else acc + t
    return acc


def _sigmoid(x):
    return 1.0 / (1.0 + jnp.exp(-x))


def _silu(x):
    return x * _sigmoid(x)


def _mod_kernel(c_ref, w_ref, b_ref, o_ref):
    cond = _silu(c_ref[...])
    o_ref[0] = _mm(cond, w_ref[0], 2, 2) + b_ref[0]


def _modulation(c, ada_w, ada_b):
    depth, d, n = ada_w.shape
    b = c.shape[0]
    tn = 1024
    return pl.pallas_call(
        _mod_kernel,
        grid=(depth, n // tn),
        in_specs=[pl.BlockSpec((b, d), lambda l, j: (0, 0)),
                  pl.BlockSpec((1, d, tn), lambda l, j: (l, 0, j)),
                  pl.BlockSpec((1, 1, tn), lambda l, j: (l, 0, j))],
        out_specs=pl.BlockSpec((1, b, tn), lambda l, j: (l, 0, j)),
        out_shape=jax.ShapeDtypeStruct((depth, b, n), F32),
        compiler_params=_cparams(("arbitrary", "arbitrary")),
        name="adaln_mod",
    )(c, ada_w, ada_b.reshape(depth, 1, n))


def _norm_proj_kernel(x_ref, g_ref, sc_ref, sh_ref, wm_ref, ws_ref, zm_ref, zs_ref, h_scr):
    @pl.when(pl.program_id(2) == 0)
    def _():
        x = x_ref[0]
        ms = jnp.mean(x * x, axis=-1, keepdims=True)
        y = x * lax.rsqrt(ms + EPS) * g_ref[...]
        h = (y * (1.0 + sc_ref[0]) + sh_ref[0]).astype(BF16)
        h_scr[...] = h
        zs_ref[0] = jnp.dot(h, ws_ref[...], preferred_element_type=F32)

    zm_ref[0] = jnp.dot(h_scr[...], wm_ref[...], preferred_element_type=F32)


def _norm_proj(x, g, scale, shift, w_main, w_small, tm, tn):
    b, s, d = x.shape
    nm = w_main.shape[1]
    ns = w_small.shape[1]
    return pl.pallas_call(
        _norm_proj_kernel,
        grid=(b, s // tm, nm // tn),
        in_specs=[pl.BlockSpec((1, tm, d), lambda bi, i, j: (bi, i, 0)),
                  pl.BlockSpec((1, d), lambda bi, i, j: (0, 0)),
                  pl.BlockSpec((1, 1, d), lambda bi, i, j: (bi, 0, 0)),
                  pl.BlockSpec((1, 1, d), lambda bi, i, j: (bi, 0, 0)),
                  pl.BlockSpec((d, tn), lambda bi, i, j: (0, j)),
                  pl.BlockSpec((d, ns), lambda bi, i, j: (0, 0))],
        out_specs=[pl.BlockSpec((1, tm, tn), lambda bi, i, j: (bi, i, j)),
                   pl.BlockSpec((1, tm, ns), lambda bi, i, j: (bi, i, 0))],
        out_shape=[jax.ShapeDtypeStruct((b, s, nm), F32),
                   jax.ShapeDtypeStruct((b, s, ns), F32)],
        scratch_shapes=[pltpu.VMEM((tm, d), BF16)],
        compiler_params=_cparams(("arbitrary", "arbitrary", "arbitrary")),
        name="norm_proj",
    )(x, g.reshape(1, d), scale.reshape(b, 1, d), shift.reshape(b, 1, d), w_main, w_small)


def _conv_jobs(bb, val_ref, glu_ref, gate_ref, w_ref, vec_ref, o_ref, u_scr, sh_scr, y_scr):
    t = val_ref.shape[1]
    ch = val_ref.shape[2]
    ext = sh_scr.shape[2]
    first = CONV_HALO - (CONV_WIDTH - 1)
    rb = min(CONV_ACC_ROWS, t)

    def prepare():
        u_scr[bb, 0:CONV_HALO, :] = u_scr[bb, t:t + CONV_HALO, :]
        u_scr[bb, CONV_HALO:CONV_HALO + t, :] = val_ref[bb] * _sigmoid(glu_ref[bb])
        for a in range(1, SUBLANES):
            sh_scr[bb, a - 1] = u_scr[bb, pl.ds(a, ext), :]

    def strip(cs):
        ls = slice(cs * LANES, (cs + 1) * LANES)
        for r0 in range(0, t, rb):
            acc = None
            for j in range(CONV_WIDTH):
                a, k8 = (first + j) % SUBLANES, (first + j) // SUBLANES * SUBLANES
                rows = slice(k8 + r0, k8 + r0 + rb)
                blk = u_scr[bb, rows, ls] if a == 0 else sh_scr[bb, a - 1, rows, ls]
                term = w_ref[j:j + 1, ls] * blk
                acc = term if acc is None else acc + term
            y_scr[bb, r0:r0 + rb, ls] = acc + vec_ref[0:1, ls]

    def finish():
        y = y_scr[bb]
        mu = jnp.mean(y, axis=-1, keepdims=True)
        yc = y - mu
        var = jnp.mean(yc * yc, axis=-1, keepdims=True)
        yn = yc * lax.rsqrt(var + LN_EPS) * vec_ref[1:2, :] + vec_ref[2:3, :]
        o_ref[bb] = (_silu(yn) * _silu(gate_ref[bb])).astype(o_ref.dtype)

    return [prepare] + [functools.partial(strip, cs) for cs in range(ch // LANES)] + [finish]


def _token_shift(x, carry_row):
    rolled = pltpu.roll(x, 1, axis=0)
    row = lax.broadcasted_iota(I32, x.shape, 0)
    return jnp.where(row == 0, carry_row, rolled)


def _mixer_kernel(r_ref, k_ref, v_ref, g_ref, lo_ref, mu_ref, mul_ref, vec_ref, rk_ref, wcat_ref,
                  tri_ref, ones_ref, cval_ref, cglu_ref, cgate_ref, cw_ref, cvec_ref, o_ref, oc_ref,
                  carry_scr, carryl_scr, h_scr, y_scr, u_scr, sh_scr, yc_scr):
    nb = r_ref.shape[0]
    c = r_ref.shape[1]
    dim = r_ref.shape[2]
    nquad = dim // QUAD

    @pl.when(pl.program_id(1) == 0)
    def _():
        carry_scr[...] = jnp.zeros(carry_scr.shape, F32)
        carryl_scr[...] = jnp.zeros(carryl_scr.shape, F32)
        h_scr[...] = jnp.zeros(h_scr.shape, F32)
        u_scr[...] = jnp.zeros(u_scr.shape, F32)

    conv_jobs = iter([job for bb in range(nb)
                      for job in _conv_jobs(bb, cval_ref, cglu_ref, cgate_ref, cw_ref, cvec_ref, oc_ref,
                                            u_scr, sh_scr, yc_scr)])

    def conv_work(n):
        for _ in range(n):
            job = next(conv_jobs, None)
            if job is not None:
                job()

    ones_bd = ones_ref[...]

    def head_sum(x, parts):
        return jnp.concatenate(
            [_mm(x[:, q * QUAD:(q + 1) * QUAD], ones_bd, parts, 1) for q in range(nquad)], axis=1)

    def prepare(bb):
        def lerp(idx, ref, mu_row):
            x = ref[bb]
            prev = _token_shift(x, carry_scr[bb, idx, 0:1, :])
            carry_scr[bb, idx, 0:1, :] = x[c - 1:c, :]
            return x + (prev - x) * mu_row

        r = lerp(0, r_ref, mu_ref[0:1, :])
        k = lerp(1, k_ref, mu_ref[1:2, :])
        v = lerp(2, v_ref, mu_ref[2:3, :])
        lo0 = lo_ref[bb]
        lo_prev = _token_shift(lo0, carryl_scr[bb, 0:1, :])
        carryl_scr[bb, 0:1, :] = lo0[c - 1:c, :]
        lo = lo0 + (lo_prev - lo0) * mul_ref[...]
        lane = lax.broadcasted_iota(I32, lo.shape, 1)
        lo_act = jnp.where(lane < LORA_DIM, jnp.tanh(lo), lo)
        pre = _mm(lo_act, wcat_ref[...], 2, 2)
        w_pre = pre[:, :dim] + vec_ref[0:1, :]
        a_pre = pre[:, dim:] + vec_ref[1:2, :]
        sp = jnp.maximum(-w_pre, 0.0) + jnp.log(1.0 + jnp.exp(-jnp.abs(w_pre)))
        logw = -jnp.exp(-sp - 0.5)
        a = _sigmoid(a_pre)
        kk = k * vec_ref[2:3, :]
        kk = kk / jnp.maximum(jnp.sqrt(head_sum(kk * kk, 2)), 1e-12)
        kmod = k * (1.0 + (a - 1.0) * vec_ref[3:4, :])
        kka = kk * a
        cum = _mm(tri_ref[...], logw, 1, 2)
        tot = cum[c - 1:c, :]
        e_neg = jnp.exp(-cum)
        e_rem = jnp.exp(tot - cum)
        return dict(r=r, v=v, kmod=kmod, at=-kk * jnp.exp(cum - logw), bt=kka * e_neg, kt=kmod * e_neg,
                    rt=r * jnp.exp(cum), bh=kka * e_rem, kh=kmod * e_rem, gam=jnp.exp(tot))

    prep = [prepare(bb) for bb in range(nb)]
    conv_work(2)

    lane_h = lax.broadcasted_iota(I32, (c, QUAD), 1) // RWKV_HEAD
    n4 = HEADS_PER_QUAD * c
    row = lax.broadcasted_iota(I32, (n4, n4), 0)
    col = lax.broadcasted_iota(I32, (n4, n4), 1)
    strict = row > col
    incl = row >= col
    diag = row == col

    def stack(z):
        return jnp.concatenate([jnp.where(lane_h == h, z, 0.0) for h in range(HEADS_PER_QUAD)], axis=0)

    def unstack(zs):
        out = zs[0:c]
        for h in range(1, HEADS_PER_QUAD):
            out = out + zs[h * c:(h + 1) * c]
        return out

    chains = [(bb, q) for bb in range(nb) for q in range(nquad)]
    ids = range(len(chains))

    def part(name, i):
        bb, q = chains[i]
        return prep[bb][name][:, q * QUAD:(q + 1) * QUAD]

    def stage(fn):
        out = []
        for i in ids:
            out.append(fn(i))
            if i % CONV_SPREAD == CONV_SPREAD - 1:
                conv_work(1)
        return out

    sa = [stack(part("at", i)) for i in ids]
    sv = [stack(part("v", i)) for i in ids]
    aa = stage(lambda i: _mm(jnp.concatenate([sa[i], stack(part("rt", i))], axis=0),
                             jnp.concatenate([stack(part("bt", i)), stack(part("kt", i))], axis=0),
                             trans_b=True))
    n_ab = [jnp.where(strict, x[0:n4, 0:n4], 0.0) for x in aa]
    a_ak = [jnp.where(strict, x[0:n4, n4:], 0.0) for x in aa]
    a_rb = [jnp.where(incl, x[n4:, 0:n4], 0.0) for x in aa]
    a_rk = [jnp.where(incl, x[n4:, n4:], 0.0) for x in aa]
    tm = [jnp.where(diag, 1.0, x) for x in n_ab]
    p = n_ab
    for _ in range(int(math.log2(c)) - 1):
        p = stage(lambda i: _mm(p[i], p[i]))
        tm = stage(lambda i: tm[i] + _mm(tm[i], p[i]))
    ta = stage(lambda i: _mm(tm[i], sa[i]))
    av = stage(lambda i: _mm(a_ak[i], sv[i]))
    tav = stage(lambda i: _mm(tm[i], av[i]))
    bs_t = [stack(part("bh", i)).T for i in ids]
    ks_t = [stack(part("kh", i)).T for i in ids]
    m_low = stage(lambda i: _mm(bs_t[i], ta[i]))
    g_new = stage(lambda i: _mm(bs_t[i], tav[i]) + _mm(ks_t[i], sv[i]))
    qmat = stage(lambda i: part("rt", i) + unstack(_mm(a_rb[i], ta[i])))
    y_in = stage(lambda i: unstack(_mm(a_rb[i], tav[i]) + _mm(a_rk[i], sv[i])))
    for i in ids:
        bb, q = chains[i]
        h0 = h_scr[bb, q]
        gam_col = jnp.sum(jnp.where(diag, part("gam", i), 0.0), axis=1, keepdims=True)
        y_scr[bb, :, q * QUAD:(q + 1) * QUAD] = y_in[i] + _mm(qmat[i], h0)
        h_scr[bb, q] = gam_col * h0 + _mm(m_low[i], h0) + g_new[i]
    conv_work(len(ids) * 8)

    inv_n = 1.0 / RWKV_HEAD
    for bb in range(nb):
        y = y_scr[bb]
        mu = head_sum(y, 1) * inv_n
        yc = y - mu
        var = head_sum(yc * yc, 1) * inv_n
        yn = yc * lax.rsqrt(var + GN_EPS) * vec_ref[4:5, :] + vec_ref[5:6, :]
        bonus = head_sum(prep[bb]["r"] * prep[bb]["kmod"] * rk_ref[...], 1) * prep[bb]["v"]
        o_ref[bb] = ((yn + bonus) * _silu(g_ref[bb])).astype(o_ref.dtype)


def _mixers(zm, zs, conv_w, conv_vec, mu_rkv, mu_lora, vec, w_up, a_up, r_k, conv_cols, cols):
    b, s, _ = zm.shape
    dim = mu_rkv.shape[1]
    c = RWKV_CHUNK
    cr, ck, cv, cg = cols
    blk = lambda col: pl.BlockSpec((nb, c, dim), lambda bi, i: (bi, i, col))
    zero = jnp.zeros((LORA_DIM, dim), F32)
    wcat = jnp.concatenate([jnp.concatenate([w_up, zero], axis=1),
                            jnp.concatenate([zero, a_up], axis=1)], axis=0)
    tri = jnp.asarray(np.tril(np.ones((c, c), np.float32)), BF16)
    hid = np.arange(QUAD) // RWKV_HEAD
    ones_bd = jnp.asarray((hid[:, None] == hid[None, :]).astype(np.float32), BF16)
    const = lambda shape: pl.BlockSpec(shape, lambda bi, i: (0,) * len(shape))
    nb = RWKV_BATCH_ROWS if b % RWKV_BATCH_ROWS == 0 else 1
    yb, ya = pl.pallas_call(
        _mixer_kernel,
        grid=(b // nb, s // c),
        in_specs=[blk(cr), blk(ck), blk(cv), blk(cg),
                  pl.BlockSpec((nb, c, 2 * LORA_DIM), lambda bi, i: (bi, i, 0)),
                  const((3, dim)), const((1, 2 * LORA_DIM)), const((6, dim)), const((1, dim)),
                  const((2 * LORA_DIM, 2 * dim)), const((c, c)), const((QUAD, QUAD)),
                  blk(conv_cols[0]), blk(conv_cols[1]), blk(conv_cols[2]),
                  const((CONV_WIDTH, dim)), const((3, dim))],
        out_specs=[blk(0), blk(0)],
        out_shape=[jax.ShapeDtypeStruct((b, s, dim), BF16), jax.ShapeDtypeStruct((b, s, dim), BF16)],
        scratch_shapes=[pltpu.VMEM((nb, 3, 8, dim), F32),
                        pltpu.VMEM((nb, 8, 2 * LORA_DIM), F32),
                        pltpu.VMEM((nb, dim // QUAD, QUAD, QUAD), F32),
                        pltpu.VMEM((nb, c, dim), F32),
                        pltpu.VMEM((nb, CONV_HALO + c, dim), F32),
                        pltpu.VMEM((nb, SUBLANES - 1, c + CONV_HALO - SUBLANES, dim), F32),
                        pltpu.VMEM((nb, c, dim), F32)],
        compiler_params=_cparams(("arbitrary", "arbitrary")),
        name="mixers_conv_rwkv7",
    )(zm, zm, zm, zm, zs, mu_rkv, mu_lora.reshape(1, 2 * LORA_DIM), vec, r_k.reshape(1, dim),
      wcat, tri, ones_bd, zm, zm, zm, conv_w, conv_vec)
    return ya, yb


def _out_proj_kernel(*refs, n_in, final):
    y_refs = refs[:n_in]
    w_refs = refs[n_in:2 * n_in]
    x_ref, gate_ref = refs[2 * n_in], refs[2 * n_in + 1]
    o_ref = refs[-1]
    acc = jnp.dot(y_refs[0][0], w_refs[0][...], preferred_element_type=F32)
    for y_ref, w_ref in zip(y_refs[1:], w_refs[1:]):
        acc = acc + jnp.dot(y_ref[0], w_ref[...], preferred_element_type=F32)
    xn = x_ref[0] + gate_ref[0] * acc
    if final:
        fg_ref = refs[2 * n_in + 2]
        ms = jnp.mean(xn * xn, axis=-1, keepdims=True)
        xn = xn * lax.rsqrt(ms + EPS) * fg_ref[...]
    o_ref[0] = xn


def _out_proj(ys, ws, x, gate, final_g, tm):
    b, s, d = x.shape
    n_in = len(ys)
    final = final_g is not None
    in_specs = [pl.BlockSpec((1, tm, y.shape[2]), lambda bi, i: (bi, i, 0)) for y in ys]
    in_specs += [pl.BlockSpec(w.shape, lambda bi, i: (0, 0)) for w in ws]
    in_specs += [pl.BlockSpec((1, tm, d), lambda bi, i: (bi, i, 0)),
                 pl.BlockSpec((1, 1, d), lambda bi, i: (bi, 0, 0))]
    args = list(ys) + list(ws) + [x, gate.reshape(b, 1, d)]
    if final:
        in_specs.append(pl.BlockSpec((1, d), lambda bi, i: (0, 0)))
        args.append(final_g.reshape(1, d))
    return pl.pallas_call(
        functools.partial(_out_proj_kernel, n_in=n_in, final=final),
        grid=(b, s // tm),
        in_specs=in_specs,
        out_specs=pl.BlockSpec((1, tm, d), lambda bi, i: (bi, i, 0)),
        out_shape=jax.ShapeDtypeStruct((b, s, d), F32),
        compiler_params=_cparams(("arbitrary", "arbitrary")),
        name="out_proj_final" if final else "out_proj",
    )(*args)


def _rope128(x, cos, sin_signed):
    return x * cos + pltpu.roll(x, 64, axis=1) * sin_signed


def _rope64_pair(x, cos, sin_signed):
    lane = lax.broadcasted_iota(I32, x.shape, 1)
    first_half = (lane % IDX_HEAD_DIM) < (IDX_HEAD_DIM // 2)
    partner = jnp.where(first_half, pltpu.roll(x, 96, axis=1), pltpu.roll(x, 32, axis=1))
    return x * cos + partner * sin_signed


def _rope_kv_kernel(zs_ref, pos_ref, inv_ref, sgn_ref, k_ref, v_ref, ki_ref, tab_ref):
    posf = pos_ref[0].astype(F32)
    ang128 = posf * inv_ref[0:1, :]
    ang64 = posf * inv_ref[1:2, :]
    c128 = jnp.cos(ang128)
    s128 = jnp.sin(ang128) * sgn_ref[0:1, :]
    c64 = jnp.cos(ang64)
    s64 = jnp.sin(ang64) * sgn_ref[1:2, :]
    tab_ref[0, 0] = c128
    tab_ref[0, 1] = s128
    tab_ref[0, 2] = c64
    tab_ref[0, 3] = s64
    zs = zs_ref[0]
    k_ref[0, 0] = _rope128(zs[:, 0:128], c128, s128).T.astype(BF16)
    v_ref[0, :, 0:128] = zs[:, 128:256].astype(BF16)
    v_ref[0, :, 128:256] = jnp.ones((zs.shape[0], 128), BF16)
    ki = _rope64_pair(zs[:, 256:384], c64, s64)
    lane = lax.broadcasted_iota(I32, ki.shape, 1)
    ki_ref[0, 0] = jnp.where(lane < IDX_HEAD_DIM, ki, pltpu.roll(ki, 64, axis=1)).T.astype(BF16)


def _rope_kv(zs, positions):
    b, s, ns = zs.shape
    t = 2 * KEY_TILE
    tile = pl.BlockSpec((1, 1, 128, t), lambda bi, i: (bi, i, 0, 0))
    inv128 = ROPE_THETA ** (-jnp.arange(0, ATT_HEAD_DIM, 2, dtype=F32) / ATT_HEAD_DIM)
    inv64 = ROPE_THETA ** (-jnp.arange(0, IDX_HEAD_DIM, 2, dtype=F32) / IDX_HEAD_DIM)
    inv = jnp.stack([jnp.tile(inv128, 2), jnp.tile(inv64, 4)])
    sgn128 = np.where(np.arange(128) < 64, -1.0, 1.0)
    sgn64 = np.where((np.arange(128) % 64) < 32, -1.0, 1.0)
    sgn = jnp.asarray(np.stack([sgn128, sgn64]), F32)
    blk = lambda w: pl.BlockSpec((1, t, w), lambda bi, i: (bi, i, 0))
    return pl.pallas_call(
        _rope_kv_kernel,
        grid=(b, s // t),
        in_specs=[blk(ns), blk(1),
                  pl.BlockSpec((2, 128), lambda bi, i: (0, 0)),
                  pl.BlockSpec((2, 128), lambda bi, i: (0, 0))],
        out_specs=[tile, blk(256), tile,
                   pl.BlockSpec((1, 4, t, 128), lambda bi, i: (bi, 0, i, 0))],
        out_shape=[jax.ShapeDtypeStruct((b, s // t, 128, t), BF16),
                   jax.ShapeDtypeStruct((b, s, 256), BF16),
                   jax.ShapeDtypeStruct((b, s // t, 128, t), BF16),
                   jax.ShapeDtypeStruct((b, 4, s, 128), F32)],
        compiler_params=_cparams(("arbitrary", "arbitrary")),
        name="rope_kv",
    )(zs, positions.reshape(b, s, 1), inv, sgn)


def _dsa_kernel(q_ref, g_ref, qi_ref, zs_ref, tab_ref, qpos_ref, kpos_ref, k_ref, v_ref, ki_ref,
                upper_ref, o_ref, qs_scr, qis_scr, wb_scr, key_scr, hi_scr, lo_scr, acc_scr, m_scr,
                *, k_sel):
    tq = q_ref.shape[1]
    qb = pl.program_id(1)
    nk = (qb * tq + tq + KEY_TILE - 1) // KEY_TILE
    npair = (nk + 1) // 2
    att_tile = 2 * KEY_TILE

    c128, s128 = tab_ref[0, 0], tab_ref[0, 1]
    c64, s64 = tab_ref[0, 2], tab_ref[0, 3]
    q_scale = (ATT_HEAD_DIM ** -0.5) * math.log2(math.e)
    lane = lax.broadcasted_iota(I32, (tq, 128), 1)
    eye = jnp.where(lax.broadcasted_iota(I32, (tq, 128), 0) == lane, 1.0, 0.0).astype(BF16)
    for h in range(ATT_HEADS):
        qh = _rope128(q_ref[0, :, h * 128:(h + 1) * 128], c128, s128) * q_scale
        qs_scr[h * tq:(h + 1) * tq, 0:128] = qh.astype(BF16)
        qs_scr[h * tq:(h + 1) * tq, 128:256] = eye
    zs = zs_ref[0]
    w_scale = (IDX_HEADS ** -0.5) * (IDX_HEAD_DIM ** -0.5)
    w_off = 256 + IDX_HEAD_DIM
    for p in range(IDX_HEADS // 2):
        pair = _rope64_pair(qi_ref[0, :, p * 128:(p + 1) * 128], c64, s64)
        qis_scr[(2 * p) * tq:(2 * p + 1) * tq, :] = jnp.where(lane < 64, pair, 0.0).astype(BF16)
        qis_scr[(2 * p + 1) * tq:(2 * p + 2) * tq, :] = jnp.where(lane >= 64, pair, 0.0).astype(BF16)
    for h in range(IDX_HEADS):
        wcol = zs[:, w_off + h:w_off + h + 1] * w_scale
        wb_scr[h * tq:(h + 1) * tq, :] = jnp.broadcast_to(wcol, (tq, 128))

    qchunk = qpos_ref[0] >> 6

    def score_body(jp, carry):
        kt = ki_ref[0, jp]
        nchunk = IDX_HEADS // 2

        def logits(c):
            return _mm(qis_scr[2 * c * tq:(2 * c + 2) * tq, :], kt)

        def weighted(c, sc):
            out = None
            for h in range(2):
                wb = jnp.concatenate([wb_scr[(2 * c + h) * tq:(2 * c + h + 1) * tq, :]] * (att_tile // 128),
                                     axis=1)
                term = jnp.maximum(sc[h * tq:(h + 1) * tq], 0.0) * wb
                out = term if out is None else out + term
            return out

        ahead = 2
        sc = {c: logits(c) for c in range(min(ahead, nchunk))}
        tot = None
        for c in range(nchunk):
            if c + ahead < nchunk:
                sc[c + ahead] = logits(c + ahead)
            w = weighted(c, sc.pop(c))
            tot = w if tot is None else tot + w
        for t in range(2):
            kchunk = kpos_ref[0, pl.ds(2 * jp + t, 1), :] >> 6
            part = jnp.where(kchunk <= qchunk, tot[:, t * KEY_TILE:(t + 1) * KEY_TILE], -jnp.inf)
            bits = pltpu.bitcast(part, I32)
            key = bits ^ ((bits >> 31) & 0x7FFFFFFF)
            key_scr[2 * jp + t] = key
            hi_scr[2 * jp + t] = (key >> 16).astype(I16)
        return carry

    lax.fori_loop(0, npair, score_body, 0)
    neg_key = INT_MIN + 0x7FFFFF
    sentinel16 = jnp.full((tq, KEY_TILE), MIN16, I16)

    @pl.when(npair % 2 == 1)
    def _():
        hi_scr[2 * npair] = sentinel16
        hi_scr[2 * npair + 1] = sentinel16

    one16 = jnp.ones((tq, KEY_TILE), I16)
    zero16 = jnp.zeros((tq, KEY_TILE), I16)

    def to16(col):
        return jnp.broadcast_to(col, (tq, KEY_TILE)).astype(I16)

    def count16(plane, cand16, strict):
        cmp = (lambda x: x > cand16) if strict else (lambda x: x >= cand16)

        def body(jq, acc):
            for t in range(4):
                acc = acc + jnp.where(cmp(plane[4 * jq + t]), one16, zero16)
            return acc

        acc = lax.fori_loop(0, (npair + 1) // 2, body, zero16)
        return jnp.sum(acc.astype(I32), axis=1, keepdims=True)

    def kth_largest16(plane, k_need):
        def bit_body(i, thr):
            cand = thr + jnp.left_shift(jnp.int32(1), 15 - i)
            cnt = count16(plane, to16(cand), False)
            return jnp.where(cnt >= k_need, cand, thr)
        return lax.fori_loop(0, 16, bit_body, jnp.full((tq, 1), MIN16, I32))

    thi = kth_largest16(hi_scr, k_sel)
    n_above = count16(hi_scr, to16(thi), True)

    def lo_body(j, carry):
        key = key_scr[j]
        lo = (key & 0xFFFF) - 32768
        lo_scr[j] = jnp.where((key >> 16) == thi, lo, MIN16).astype(I16)
        return carry

    lax.fori_loop(0, 2 * npair, lo_body, 0)

    @pl.when(npair % 2 == 1)
    def _():
        lo_scr[2 * npair] = sentinel16
        lo_scr[2 * npair + 1] = sentinel16
    tlo = kth_largest16(lo_scr, k_sel - n_above)
    n_gt = n_above + count16(lo_scr, to16(tlo), True)
    thr = thi * 65536 + (tlo + 32768)
    need = (k_sel - n_gt).astype(F32)

    acc_scr[...] = jnp.zeros(acc_scr.shape, F32)
    m_scr[...] = jnp.full(m_scr.shape, NEG_BIG, F32)
    upper = upper_ref[...]
    hpc = ATT_CHUNK_HEADS
    nchunk = ATT_HEADS // hpc
    crow = hpc * tq

    def mask_bias(step, eqc):
        st = jnp.minimum(step, npair - 1)
        parts = []
        for t in range(2):
            key = key_scr[2 * st + t]
            eqf = jnp.where(key == thr, 1.0, 0.0)
            before = _mm(eqf.astype(BF16), upper) + eqc
            take = jnp.where(key > thr, 1.0, jnp.where(before < need, eqf, 0.0))
            parts.append(jnp.where((take > 0.0) & (key > neg_key), 0.0, NEG_BIG).astype(BF16))
            eqc = eqc + jnp.sum(eqf, axis=1, keepdims=True)
        return jnp.concatenate(parts, axis=1), eqc

    def att_body(jp, carry):
        eqc, bias = carry
        rhs = jnp.concatenate([k_ref[0, jp], bias], axis=0)
        vt = v_ref[0, pl.ds(pl.multiple_of(jp * att_tile, att_tile), att_tile), :]

        def logits(c):
            return _mm(qs_scr[c * crow:(c + 1) * crow, :], rhs)

        def softmax(c, s):
            ps, alphas = [], []
            for h in range(hpc):
                rs = slice((hpc * c + h) * tq, (hpc * c + h + 1) * tq)
                sh = s[h * tq:(h + 1) * tq].astype(BF16)
                m_old = m_scr[rs, :]
                tile_max = sh[:, 0:128]
                for t in range(1, att_tile // 128):
                    tile_max = jnp.maximum(tile_max, sh[:, t * 128:(t + 1) * 128])
                m_new = jnp.maximum(m_old, jnp.max(tile_max.astype(F32), axis=1, keepdims=True))
                ps.append(jnp.exp2(sh - jnp.concatenate([m_new.astype(BF16)] * (att_tile // 128), axis=1)))
                alphas.append(jnp.exp2(m_old - m_new))
                m_scr[rs, :] = m_new
            return jnp.concatenate(ps, axis=0), jnp.concatenate(alphas, axis=0)

        def accumulate(c, p, alpha):
            rs = slice(c * crow, (c + 1) * crow)
            acc_scr[rs, :] = jnp.concatenate([alpha, alpha], axis=1) * acc_scr[rs, :] + _mm(p, vt)

        ahead = 2
        s = {c: logits(c) for c in range(min(ahead, nchunk))}
        prev = None
        for c in range(nchunk):
            if c + ahead < nchunk:
                s[c + ahead] = logits(c + ahead)
            cur = softmax(c, s.pop(c))
            if prev is not None:
                accumulate(c - 1, *prev)
            if c == 0:
                bias_next, eqc = mask_bias(jp + 1, eqc)
            prev = cur
        accumulate(nchunk - 1, *prev)
        return eqc, bias_next

    bias0, eqc0 = mask_bias(0, jnp.zeros((tq, 1), F32))
    lax.fori_loop(0, npair, att_body, (eqc0, bias0))
    o = acc_scr[:, 0:128] / acc_scr[:, 128:256]
    for h in range(ATT_HEADS):
        sl = slice(h * 128, (h + 1) * 128)
        o_ref[0, :, sl] = (o[h * tq:(h + 1) * tq, :] * _silu(g_ref[0, :, sl])).astype(o_ref.dtype)


def _dsa(zm, zs, tabs, positions, k_rot, v_b, ki2):
    b, s, _ = zm.shape
    tq = Q_BLOCK
    att = ATT_HEADS * ATT_HEAD_DIM
    idx = IDX_HEADS * IDX_HEAD_DIM
    k_sel = min(TOPK_MAX, s // 4)
    nkt = s // KEY_TILE
    assert nkt % 2 == 0, "key tiles are consumed in pairs"
    rows = ATT_HEADS * tq
    upper = jnp.asarray(np.triu(np.ones((KEY_TILE, KEY_TILE), np.float32), 1), BF16)
    tiles = pl.BlockSpec((1, nkt // 2, 128, 2 * KEY_TILE), lambda bi, i: (bi, 0, 0, 0))
    return pl.pallas_call(
        functools.partial(_dsa_kernel, k_sel=k_sel),
        grid=(b, s // tq),
        in_specs=[pl.BlockSpec((1, tq, att), lambda bi, i: (bi, i, 0)),
                  pl.BlockSpec((1, tq, att), lambda bi, i: (bi, i, 1)),
                  pl.BlockSpec((1, tq, idx), lambda bi, i: (bi, i, 2 * att // idx)),
                  pl.BlockSpec((1, tq, zs.shape[2]), lambda bi, i: (bi, i, 0)),
                  pl.BlockSpec((1, 4, tq, 128), lambda bi, i: (bi, 0, i, 0)),
                  pl.BlockSpec((1, tq, 1), lambda bi, i: (bi, i, 0)),
                  pl.BlockSpec((1, nkt, KEY_TILE), lambda bi, i: (bi, 0, 0)),
                  tiles, pl.BlockSpec((1, s, 256), lambda bi, i: (bi, 0, 0)), tiles,
                  pl.BlockSpec((KEY_TILE, KEY_TILE), lambda bi, i: (0, 0))],
        out_specs=pl.BlockSpec((1, tq, att), lambda bi, i: (bi, i, 0)),
        out_shape=jax.ShapeDtypeStruct((b, s, att), BF16),
        scratch_shapes=[pltpu.VMEM((rows, 256), BF16),
                        pltpu.VMEM((IDX_HEADS * tq, 128), BF16),
                        pltpu.VMEM((IDX_HEADS * tq, 128), F32),
                        pltpu.VMEM((nkt, tq, KEY_TILE), I32),
                        pltpu.VMEM((nkt + 2, tq, KEY_TILE), I16),
                        pltpu.VMEM((nkt + 2, tq, KEY_TILE), I16),
                        pltpu.VMEM((rows, 256), F32),
                        pltpu.VMEM((rows, 128), F32)],
        compiler_params=_cparams(("arbitrary", "arbitrary")),
        name="dsa_attention",
    )(zm, zm, zm, zs, tabs, positions.reshape(b, s, 1), positions.reshape(b, nkt, KEY_TILE),
      k_rot, v_b, ki2, upper)


def _even_layer(x, scale, shift, gate, g, w_in, w_out, conv_w, conv_vec, mu_rkv, mu_lora, vec,
                w_up, a_up, r_k, final_g):
    d = x.shape[2]
    n6 = 6 * d
    lora = 2 * LORA_DIM
    w_main = jnp.concatenate([w_in[:, :n6], w_in[:, n6 + lora:]], axis=1).astype(BF16)
    w_small = w_in[:, n6:n6 + lora].astype(BF16)
    zm, zs = _norm_proj(x, g, scale, shift, w_main, w_small, tm=min(PROJ_ROWS, x.shape[1]), tn=1792)
    ya, yb = _mixers(zm, zs, conv_w[:, 0, :], conv_vec, mu_rkv, mu_lora, vec, w_up, a_up, r_k,
                     conv_cols=(0, 1, 2), cols=(3, 4, 5, 6))
    wo = w_out.astype(BF16)
    return _out_proj([ya, yb], [wo[:d], wo[d:]], x, gate, final_g, tm=min(OUT_ROWS, x.shape[1]))


def _odd_layer(x, scale, shift, gate, g, w_in, w_out, positions, final_g):
    att = ATT_HEADS * ATT_HEAD_DIM
    idx = IDX_HEADS * IDX_HEAD_DIM
    o_q, o_k, o_v = 0, att, att + 128
    o_qi = o_v + 128
    o_ki = o_qi + idx
    o_wi = o_ki + IDX_HEAD_DIM
    o_g = o_wi + IDX_HEADS
    w_main = jnp.concatenate([w_in[:, o_q:o_k], w_in[:, o_g:], w_in[:, o_qi:o_ki]], axis=1).astype(BF16)
    pad = jnp.zeros((w_in.shape[0], 128 - IDX_HEAD_DIM - IDX_HEADS), w_in.dtype)
    w_small = jnp.concatenate([w_in[:, o_k:o_qi], w_in[:, o_ki:o_g], pad], axis=1).astype(BF16)
    zm, zs = _norm_proj(x, g, scale, shift, w_main, w_small, tm=min(PROJ_ROWS, x.shape[1]), tn=2304)
    k_rot, v_b, ki2, tabs = _rope_kv(zs, positions)
    o = _dsa(zm, zs, tabs, positions, k_rot, v_b, ki2)
    return _out_proj([o], [w_out.astype(BF16)], x, gate, final_g, tm=min(OUT_ROWS, x.shape[1]))


def kernel(x, c, positions, ada_w, ada_b, norm_g, final_g, even_w_in, even_w_out, conv_w, conv_vec,
           rwkv_mu_rkv, rwkv_mu_lora, rwkv_vec, rwkv_w_up, rwkv_a_up, rwkv_r_k, odd_w_in, odd_w_out):
    depth = ada_w.shape[0]
    d = x.shape[2]
    mod = _modulation(c, ada_w, ada_b)
    for l in range(depth):
        shift, scale, gate = mod[l, :, :d], mod[l, :, d:2 * d], mod[l, :, 2 * d:]
        fg = final_g if l == depth - 1 else None
        j = l // 2
        if l % 2 == 0:
            x = _even_layer(x, scale, shift, gate, norm_g[l], even_w_in[j], even_w_out[j], conv_w[j],
                            conv_vec[j], rwkv_mu_rkv[j], rwkv_mu_lora[j], rwkv_vec[j], rwkv_w_up[j],
                            rwkv_a_up[j], rwkv_r_k[j], fg)
        else:
            x = _odd_layer(x, scale, shift, gate, norm_g[l], odd_w_in[j], odd_w_out[j], positions, fg)
    return x
```
